```python
import math
import jax, jax.numpy as jnp
from jax import lax
import numpy as np

D_MODEL = 1024
BATCH = 32
SEQ = 256
DEPTH = 1
DEC_BATCH = 2
DEC_SEQ = 1024
PAST_LEN = 512

GRID_W = 64
MIX_WIDTH = D_MODEL
ATTN_WIDTH = MIX_WIDTH // 2
CONV_WIDTH = MIX_WIDTH - ATTN_WIDTH
N_HEADS = 4
HEAD_DIM = ATTN_WIDTH // (2 * N_HEADS)
V_DIM = 2 * HEAD_DIM
CONV_K = 3
D_FF = 2816
ROPE_BASE = 10000.0
EPS = 1e-6
Q_BLOCK = 128
N_SUB = 3
IN_WIDTH = 3 * ATTN_WIDTH + 3 * CONV_WIDTH

kernel_name = "hybrid_diffattn_shortconv_prefix_dit_step"


def rmsnorm(x, g):
    xf = x.astype(jnp.float32)
    y = xf * lax.rsqrt(jnp.mean(xf * xf, axis=-1, keepdims=True) + EPS)
    return (y * g.astype(jnp.float32)).astype(x.dtype)


def modulation(cond, w_mod, b_mod):
    m = jax.nn.silu(cond) @ w_mod + b_mod
    return m.reshape(cond.shape[0], N_SUB, 3, D_MODEL)


def swiglu(u, w_up, w_down):
    a, b = jnp.split(u @ w_up, 2, axis=-1)
    return (jax.nn.silu(a) * b) @ w_down


def short_conv(u, w):
    T = u.shape[1]
    up = jnp.pad(u, ((0, 0), (1, 1), (0, 0)))
    return up[:, :T] * w[0] + up[:, 1:T + 1] * w[1] + up[:, 2:] * w[2]


def _rot_half(x, cos, sin):
    x1, x2 = jnp.split(x, 2, axis=-1)
    return jnp.concatenate([x1 * cos - x2 * sin, x2 * cos + x1 * sin], axis=-1)


def axial_rope(x):
    T = x.shape[1]
    rows = T // GRID_W
    row = jnp.repeat(jnp.arange(rows, dtype=jnp.float32), GRID_W)
    col = jnp.tile(jnp.arange(GRID_W, dtype=jnp.float32), rows)
    half = HEAD_DIM // 2
    freqs = ROPE_BASE ** (-jnp.arange(0, half, 2, dtype=jnp.float32) / half)
    ang_r = (row[:, None] * freqs)[None, :, None, None, :]
    ang_c = (col[:, None] * freqs)[None, :, None, None, :]
    dt = x.dtype
    xr = _rot_half(x[..., :half], jnp.cos(ang_r).astype(dt), jnp.sin(ang_r).astype(dt))
    xc = _rot_half(x[..., half:], jnp.cos(ang_c).astype(dt), jnp.sin(ang_c).astype(dt))
    return jnp.concatenate([xr, xc], axis=-1)


def diff_attention(q, k, v, lam):
    B, Tq = q.shape[0], q.shape[1]
    qb = math.gcd(Q_BLOCK, Tq)
    nb = Tq // qb
    scale = HEAD_DIM ** -0.5

    def block(qblk):
        s = jnp.einsum('bqhid,bkhid->bhiqk', qblk, k).astype(jnp.float32) * scale
        p = jax.nn.softmax(s, axis=-1)
        pd = (p[:, :, 0] - lam * p[:, :, 1]).astype(v.dtype)
        return jnp.einsum('bhqk,bkhe->bqhe', pd, v)

    qs = q.reshape(B, nb, qb, N_HEADS, 2, HEAD_DIM).swapaxes(0, 1)
    o = lax.map(block, qs)
    return o.swapaxes(0, 1).reshape(B, Tq, N_HEADS, V_DIM)


def mixer(u, w_in, conv_w, lam_qk, subln_g, w_o, lambda_init, ctx_k, ctx_v):
    N, T, _ = u.shape
    A, C = ATTN_WIDTH, CONV_WIDTH
    proj = u @ w_in
    q, k, v, bg, cg, xc = jnp.split(proj, [A, 2 * A, 3 * A, 3 * A + C, 3 * A + 2 * C], axis=-1)
    q = q.reshape(N, T, N_HEADS, 2, HEAD_DIM)
    k = k.reshape(N, T, N_HEADS, 2, HEAD_DIM)
    v = v.reshape(N, T, N_HEADS, V_DIM)
    lq = lam_qk.astype(jnp.float32)
    lam = jnp.exp(jnp.sum(lq[0] * lq[1])) - jnp.exp(jnp.sum(lq[2] * lq[3])) + lambda_init
    if ctx_k is None:
        k_all, v_all = k, v
    else:
        q = axial_rope(q)
        k_lat = axial_rope(k)
        L = ctx_k.shape[1]
        k_all = jnp.concatenate([ctx_k.reshape(N, L, N_HEADS, 2, HEAD_DIM), k_lat], axis=1)
        v_all = jnp.concatenate([ctx_v, v], axis=1)
    o = diff_attention(q, k_all, v_all, lam)
    o = rmsnorm(o, subln_g) * (1.0 - lambda_init)
    conv_out = bg * short_conv(cg * xc, conv_w)
    out = jnp.concatenate([o.reshape(N, T, A), conv_out], axis=-1) @ w_o
    return out, k.reshape(N, T, N_HEADS, 2 * HEAD_DIM), v


def trunk_layer(x, mod, ln_pre, ln_post, f1u, f1d, f2u, f2d, w_in, conv_w, lam_qk, subln_g, w_o,
                lambda_init, ctx_k, ctx_v):
    def smg(i):
        return mod[:, i, 0, None, :], mod[:, i, 1, None, :], mod[:, i, 2, None, :]

    sh, sc, gt = smg(0)
    u = rmsnorm(x, ln_pre[0]) * (1.0 + sc) + sh
    x = x + 0.5 * gt * rmsnorm(swiglu(u, f1u, f1d), ln_post[0])

    sh, sc, gt = smg(1)
    u = rmsnorm(x, ln_pre[1]) * (1.0 + sc) + sh
    mix_out, k, v = mixer(u, w_in, conv_w, lam_qk, subln_g, w_o, lambda_init, ctx_k, ctx_v)
    x = x + gt * rmsnorm(mix_out, ln_post[1])

    sh, sc, gt = smg(2)
    u = rmsnorm(x, ln_pre[2]) * (1.0 + sc) + sh
    x = x + 0.5 * gt * rmsnorm(swiglu(u, f2u, f2d), ln_post[2])
    return x, k, v


def setup_inputs(seed: int = 0) -> dict:
    key = jax.random.key(seed)
    ks = jax.random.split(key, 20)
    f32 = jnp.float32

    def nrm(k, shape, s):
        return jax.random.normal(k, shape, f32) * s

    return {
        "x_prompt": nrm(ks[0], (BATCH, SEQ, D_MODEL), 1.0),
        "x_sample": nrm(ks[1], (DEC_BATCH, DEC_SEQ, D_MODEL), 1.0),
        "c": nrm(ks[2], (DEC_BATCH, D_MODEL), 1.0),
        "cache_k": nrm(ks[3], (DEC_BATCH, DEPTH, PAST_LEN, N_HEADS, 2 * HEAD_DIM), 1.0),
        "cache_v": nrm(ks[4], (DEC_BATCH, DEPTH, PAST_LEN, N_HEADS, V_DIM), 1.0),
        "c_ctx": nrm(ks[5], (D_MODEL,), 1.0),
        "w_mod": nrm(ks[6], (DEPTH, D_MODEL, N_SUB * 3 * D_MODEL), 0.5 * D_MODEL ** -0.5),
        "b_mod": nrm(ks[7], (DEPTH, N_SUB * 3 * D_MODEL), 0.02),
        "norm_pre": 1.0 + nrm(ks[8], (DEPTH, N_SUB, D_MODEL), 0.02),
        "norm_post": 1.0 + nrm(ks[9], (DEPTH, N_SUB, D_MODEL), 0.02),
        "ffn1_up": nrm(ks[10], (DEPTH, D_MODEL, 2 * D_FF), D_MODEL ** -0.5),
        "ffn1_down": nrm(ks[11], (DEPTH, D_FF, D_MODEL), D_FF ** -0.5),
        "ffn2_up": nrm(ks[12], (DEPTH, D_MODEL, 2 * D_FF), D_MODEL ** -0.5),
        "ffn2_down": nrm(ks[13], (DEPTH, D_FF, D_MODEL), D_FF ** -0.5),
        "w_in": nrm(ks[14], (DEPTH, D_MODEL, IN_WIDTH), D_MODEL ** -0.5),
        "conv_w": nrm(ks[15], (DEPTH, CONV_K, CONV_WIDTH), CONV_K ** -0.5),
        "lam_qk": nrm(ks[16], (DEPTH, 4, HEAD_DIM), 0.1),
        "subln_g": 1.0 + nrm(ks[17], (DEPTH, V_DIM), 0.02),
        "w_o": nrm(ks[18], (DEPTH, MIX_WIDTH, D_MODEL), MIX_WIDTH ** -0.5),
    }


def reference(x_prompt, x_sample, c, cache_k, cache_v, c_ctx, w_mod, b_mod, norm_pre, norm_post,
              ffn1_up, ffn1_down, ffn2_up, ffn2_down, w_in, conv_w, lam_qk, subln_g, w_o):
    h = x_prompt
    new_k, new_v = [], []
    for l in range(DEPTH):
        lambda_init = 0.8 - 0.6 * math.exp(-0.3 * l)
        mod = modulation(c_ctx[None, :], w_mod[l], b_mod[l])
        h, k_l, v_l = trunk_layer(h, mod, norm_pre[l], norm_post[l], ffn1_up[l], ffn1_down[l],
                                  ffn2_up[l], ffn2_down[l], w_in[l], conv_w[l], lam_qk[l],
                                  subln_g[l], w_o[l], lambda_init, None, None)
        new_k.append(k_l)
        new_v.append(v_l)
    y_prompt = h

    h = x_sample
    for l in range(DEPTH):
        lambda_init = 0.8 - 0.6 * math.exp(-0.3 * l)
        mod = modulation(c, w_mod[l], b_mod[l])
        h, _, _ = trunk_layer(h, mod, norm_pre[l], norm_post[l], ffn1_up[l], ffn1_down[l],
                              ffn2_up[l], ffn2_down[l], w_in[l], conv_w[l], lam_qk[l],
                              subln_g[l], w_o[l], lambda_init, cache_k[:, l], cache_v[:, l])
    y_sample = h
    return (y_prompt, y_sample, jnp.stack(new_k, axis=1), jnp.stack(new_v, axis=1))
```

```python
import functools
import math

import jax
import jax.numpy as jnp
from jax import lax
from jax.experimental import pallas as pl
from jax.experimental.pallas import tpu as pltpu

D_MODEL = 1024
SEQ = 256
DEC_SEQ = 1024
PAST_LEN = 512
GRID_W = 64
ATTN_WIDTH = 512
CONV_WIDTH = 512
N_HEADS = 4
HEAD_DIM = 64
V_DIM = 128
D_FF = 2816
ROPE_BASE = 10000.0
EPS = 1e-6
N_SUB = 3
IN_WIDTH = 3 * ATTN_WIDTH + 3 * CONV_WIDTH
LAMBDA_INIT = 0.8 - 0.6 * math.exp(-0.3 * 0)
QK_SCALE = HEAD_DIM ** -0.5

F32 = jnp.float32
BF16 = jnp.bfloat16

COND_ROWS = 8
FFN_TM = 512
FFN_CHUNKS = ((0, 1024), (1024, 2048), (2048, D_FF))
CTX_TM = 512
LAT_QB = 256
VMEM_LIMIT = 56 * 1024 * 1024


def _rms(x, g):
    ms = jnp.mean(x * x, axis=-1, keepdims=True)
    return x * lax.rsqrt(ms + EPS) * g


def _dot(a, b):
    return jnp.dot(a, b, preferred_element_type=F32)


def _dot_nt(a, b):
    return lax.dot_general(a, b, (((1,), (1,)), ((), ())), preferred_element_type=F32)


def _const_spec(shape):
    nd = len(shape)
    return pl.BlockSpec(shape, lambda *_: (0,) * nd, pipeline_mode=pl.Buffered(1))


def _mod_kernel(c_ref, w_ref, b_ref, o_ref):
    c = c_ref[...]
    s = (c * jax.nn.sigmoid(c)).astype(BF16)
    o_ref[...] = _dot(s, w_ref[...].astype(BF16)) + b_ref[...]


def _modulation(cond, w_mod, b_mod):
    n_out = w_mod.shape[1]
    bn = 1024
    return pl.pallas_call(
        _mod_kernel,
        grid=(n_out // bn,),
        in_specs=[
            pl.BlockSpec((COND_ROWS, D_MODEL), lambda j: (0, 0)),
            pl.BlockSpec((D_MODEL, bn), lambda j: (0, j)),
            pl.BlockSpec((1, bn), lambda j: (0, j)),
        ],
        out_specs=pl.BlockSpec((COND_ROWS, bn), lambda j: (0, j)),
        out_shape=jax.ShapeDtypeStruct((COND_ROWS, n_out), F32),
        name="modulation",
    )(cond, w_mod, b_mod)


def _ffn_kernel(x_ref, mod_ref, gpre_ref, gpost_ref, wup_ref, wdn_ref, o_ref, u_ref, acc_ref, *, sub):
    x = x_ref[...]
    sh = mod_ref[0, 3 * sub:3 * sub + 1, :]
    sc = mod_ref[0, 3 * sub + 1:3 * sub + 2, :]
    gt = mod_ref[0, 3 * sub + 2:3 * sub + 3, :]
    u = _rms(x, gpre_ref[sub:sub + 1, :]) * (1.0 + sc) + sh
    u_ref[...] = u.astype(BF16)
    for ci, (s, e) in enumerate(FFN_CHUNKS):
        ub = u_ref[...]
        a = _dot(ub, wup_ref[:, s:e])
        b = _dot(ub, wup_ref[:, D_FF + s:D_FF + e])
        h = (a * jax.nn.sigmoid(a) * b).astype(BF16)
        part = _dot(h, wdn_ref[s:e, :])
        if ci == 0:
            acc_ref[...] = part
        else:
            acc_ref[...] += part
    y = acc_ref[...]
    o_ref[...] = x + 0.5 * gt * _rms(y, gpost_ref[sub:sub + 1, :])


def _ffn(x2d, mod3, gpre, gpost, wup, wdn, *, sub, row_of_tile):
    n = x2d.shape[0]
    tm = FFN_TM
    return pl.pallas_call(
        functools.partial(_ffn_kernel, sub=sub),
        grid=(n // tm,),
        in_specs=[
            pl.BlockSpec((tm, D_MODEL), lambda t: (t, 0)),
            pl.BlockSpec((1, 3 * N_SUB, D_MODEL), lambda t: (row_of_tile(t), 0, 0)),
            _const_spec((N_SUB, D_MODEL)),
            _const_spec((N_SUB, D_MODEL)),
            _const_spec((D_MODEL, 2 * D_FF)),
            _const_spec((D_FF, D_MODEL)),
        ],
        out_specs=pl.BlockSpec((tm, D_MODEL), lambda t: (t, 0)),
        out_shape=jax.ShapeDtypeStruct((n, D_MODEL), F32),
        scratch_shapes=[pltpu.VMEM((tm, D_MODEL), BF16), pltpu.VMEM((tm, D_MODEL), F32)],
        compiler_params=pltpu.CompilerParams(vmem_limit_bytes=VMEM_LIMIT),
        name=f"ffn{sub}",
    )(x2d, mod3, gpre, gpost, wup, wdn)


def _lambda(lam_ref):
    lq = lam_ref[...]
    l1 = jnp.sum(lq[0:1, :] * lq[1:2, :], axis=-1, keepdims=True)
    l2 = jnp.sum(lq[2:3, :] * lq[3:4, :], axis=-1, keepdims=True)
    return jnp.exp(l1) - jnp.exp(l2) + LAMBDA_INIT


def _stack_maps(qh):
    lane = lax.broadcasted_iota(jnp.int32, qh.shape, 1)
    q0 = jnp.where(lane < HEAD_DIM, qh, 0.0).astype(BF16)
    q1 = jnp.where(lane >= HEAD_DIM, qh, 0.0).astype(BF16)
    return jnp.concatenate([q0, q1], axis=0)


def _diff_attn_head(qs, kh, vh, lam, subg):
    m = qs.shape[0] // 2
    s = _dot_nt(qs, kh)
    e = jnp.exp(s - jnp.max(s, axis=-1, keepdims=True))
    p = e * (1.0 / jnp.sum(e, axis=-1, keepdims=True))
    pd = (p[:m] - lam * p[m:]).astype(BF16)
    o = _dot(pd, vh)
    return _rms(o, subg) * (1.0 - LAMBDA_INIT)


def _short_conv(bg, cg, xc, convw_ref, seq):
    g = cg * xc
    m = g.shape[0]
    row = lax.broadcasted_iota(jnp.int32, g.shape, 0) % seq
    g_prev = jnp.where(row == 0, 0.0, pltpu.roll(g, 1, 0))
    g_next = jnp.where(row == seq - 1, 0.0, pltpu.roll(g, m - 1, 0))
    conv = g_prev * convw_ref[0:1, :] + g * convw_ref[1:2, :] + g_next * convw_ref[2:3, :]
    return bg * conv


def _pre_mod(x, mod_ref, gpre_ref, sub):
    sh = mod_ref[0, 3 * sub:3 * sub + 1, :]
    sc = mod_ref[0, 3 * sub + 1:3 * sub + 2, :]
    return _rms(x, gpre_ref[sub:sub + 1, :]) * (1.0 + sc) + sh


def _mix_ctx_kernel(x_ref, mod_ref, gpre_ref, gpost_ref, win_ref, convw_ref, lam_ref, subg_ref, wo_ref,
                    o_ref, k_ref, v_ref, u_ref, q_ref, cat_ref):
    A = ATTN_WIDTH
    x = x_ref[...]
    u_ref[...] = _pre_mod(x, mod_ref, gpre_ref, 1).astype(BF16)
    q_ref[...] = _dot(u_ref[...], win_ref[:, 0:A]) * QK_SCALE
    k_ref[...] = _dot(u_ref[...], win_ref[:, A:2 * A])
    v_ref[...] = _dot(u_ref[...], win_ref[:, 2 * A:3 * A])
    lam = _lambda(lam_ref)
    subg = subg_ref[...]
    for b in range(CTX_TM // SEQ):
        rows = slice(b * SEQ, (b + 1) * SEQ)
        for h in range(N_HEADS):
            cols = slice(h * V_DIM, (h + 1) * V_DIM)
            qs = _stack_maps(q_ref[rows, cols])
            oh = _diff_attn_head(qs, k_ref[rows, cols].astype(BF16), v_ref[rows, cols].astype(BF16), lam, subg)
            cat_ref[rows, cols] = oh.astype(BF16)
    ub = u_ref[...]
    bg = _dot(ub, win_ref[:, 3 * A:3 * A + CONV_WIDTH])
    cg = _dot(ub, win_ref[:, 3 * A + CONV_WIDTH:3 * A + 2 * CONV_WIDTH])
    xc = _dot(ub, win_ref[:, 3 * A + 2 * CONV_WIDTH:])
    cat_ref[:, A:] = _short_conv(bg, cg, xc, convw_ref, SEQ).astype(BF16)
    mix = _dot(cat_ref[...], wo_ref[...])
    gt = mod_ref[0, 5:6, :]
    o_ref[...] = x + gt * _rms(mix, gpost_ref[1:2, :])


def _mix_ctx(x2d, mod3, gpre, gpost, win, convw, lamqk, subg, wo):
    n = x2d.shape[0]
    tm = CTX_TM
    row = pl.BlockSpec((tm, D_MODEL), lambda t: (t, 0))
    kv = pl.BlockSpec((tm, ATTN_WIDTH), lambda t: (t, 0))
    return pl.pallas_call(
        _mix_ctx_kernel,
        grid=(n // tm,),
        in_specs=[
            row,
            pl.BlockSpec((1, 3 * N_SUB, D_MODEL), lambda t: (0, 0, 0)),
            _const_spec((N_SUB, D_MODEL)),
            _const_spec((N_SUB, D_MODEL)),
            _const_spec((D_MODEL, IN_WIDTH)),
            _const_spec((3, CONV_WIDTH)),
            _const_spec((4, HEAD_DIM)),
            _const_spec((1, V_DIM)),
            _const_spec((D_MODEL, D_MODEL)),
        ],
        out_specs=[row, kv, kv],
        out_shape=[
            jax.ShapeDtypeStruct((n, D_MODEL), F32),
            jax.ShapeDtypeStruct((n, ATTN_WIDTH), F32),
            jax.ShapeDtypeStruct((n, ATTN_WIDTH), F32),
        ],
        scratch_shapes=[
            pltpu.VMEM((tm, D_MODEL), BF16),
            pltpu.VMEM((tm, ATTN_WIDTH), F32),
            pltpu.VMEM((tm, D_MODEL), BF16),
        ],
        compiler_params=pltpu.CompilerParams(vmem_limit_bytes=VMEM_LIMIT),
        name="mixer_ctx",
    )(x2d, mod3, gpre, gpost, win, convw, lamqk, subg, wo)


def _rope(xh, cos, sin_lo, sin_hi):
    return xh * cos + pltpu.roll(xh, 16, 1) * sin_hi + pltpu.roll(xh, V_DIM - 16, 1) * sin_lo


def _mix_lat_kernel(x_ref, mod_ref, gpre_ref, gpost_ref, win_ref, convw_ref, lam_ref, subg_ref, wo_ref,
                    ck_ref, cv_ref, cos_ref, slo_ref, shi_ref,
                    o_ref, u_ref, qkv_ref, qs_ref, kall_ref, vall_ref, cat_ref):
    A = ATTN_WIDTH
    T = DEC_SEQ
    L = PAST_LEN
    x = x_ref[...]
    u_ref[...] = _pre_mod(x, mod_ref, gpre_ref, 1).astype(BF16)
    qkv_ref[...] = _dot(u_ref[...], win_ref[:, 0:3 * A])
    lam = _lambda(lam_ref)
    subg = subg_ref[...]
    cos, slo, shi = cos_ref[...], slo_ref[...], shi_ref[...]
    nqb = T // LAT_QB
    for h in range(N_HEADS):
        cols = slice(h * V_DIM, (h + 1) * V_DIM)
        qh = _rope(qkv_ref[:, h * V_DIM:(h + 1) * V_DIM], cos, slo, shi) * QK_SCALE
        lane = lax.broadcasted_iota(jnp.int32, qh.shape, 1)
        qs_ref[0] = jnp.where(lane < HEAD_DIM, qh, 0.0).astype(BF16)
        qs_ref[1] = jnp.where(lane >= HEAD_DIM, qh, 0.0).astype(BF16)
        kall_ref[0:L, :] = ck_ref[0, :, cols].astype(BF16)
        kall_ref[L:, :] = _rope(qkv_ref[:, A + h * V_DIM:A + (h + 1) * V_DIM], cos, slo, shi).astype(BF16)
        vall_ref[0:L, :] = cv_ref[0, :, cols].astype(BF16)
        vall_ref[L:, :] = qkv_ref[:, 2 * A + h * V_DIM:2 * A + (h + 1) * V_DIM].astype(BF16)

        def qblock(i, carry):
            r0 = pl.multiple_of(i * LAT_QB, LAT_QB)
            qs = jnp.concatenate([qs_ref[0, pl.ds(r0, LAT_QB), :], qs_ref[1, pl.ds(r0, LAT_QB), :]], axis=0)
            oh = _diff_attn_head(qs, kall_ref[...], vall_ref[...], lam, subg)
            cat_ref[pl.ds(r0, LAT_QB), cols] = oh.astype(BF16)
            return carry

        lax.fori_loop(0, nqb, qblock, 0)
    ub = u_ref[...]
    bg = _dot(ub, win_ref[:, 3 * A:3 * A + CONV_WIDTH])
    cg = _dot(ub, win_ref[:, 3 * A + CONV_WIDTH:3 * A + 2 * CONV_WIDTH])
    xc = _dot(ub, win_ref[:, 3 * A + 2 * CONV_WIDTH:])
    cat_ref[:, A:] = _short_conv(bg, cg, xc, convw_ref, T).astype(BF16)
    mix = _dot(cat_ref[...], wo_ref[...])
    gt = mod_ref[0, 5:6, :]
    o_ref[...] = x + gt * _rms(mix, gpost_ref[1:2, :])


def _rope_tables():
    t = jnp.arange(DEC_SEQ, dtype=jnp.int32)
    row = (t // GRID_W).astype(F32)
    col = (t % GRID_W).astype(F32)
    half = HEAD_DIM // 2
    freqs = ROPE_BASE ** (-jnp.arange(0, half, 2, dtype=F32) / half)
    ang = jnp.concatenate([row[:, None] * freqs, row[:, None] * freqs,
                           col[:, None] * freqs, col[:, None] * freqs], axis=-1)
    cos, sin = jnp.cos(ang), jnp.sin(ang)
    lane = jnp.arange(HEAD_DIM) % half
    first = (lane < half // 2)[None, :]
    sin_lo = jnp.where(first, -sin, 0.0)
    sin_hi = jnp.where(first, 0.0, sin)
    tile = lambda a: jnp.concatenate([a, a], axis=-1)
    return tile(cos), tile(sin_lo), tile(sin_hi)


def _mix_lat(x2d, mod3, gpre, gpost, win, convw, lamqk, subg, wo, ck, cv):
    n = x2d.shape[0]
    T = DEC_SEQ
    cos, slo, shi = _rope_tables()
    row = pl.BlockSpec((T, D_MODEL), lambda b: (b, 0))
    cache = pl.BlockSpec((1, PAST_LEN, ATTN_WIDTH), lambda b: (b, 0, 0))
    return pl.pallas_call(
        _mix_lat_kernel,
        grid=(n // T,),
        in_specs=[
            row,
            pl.BlockSpec((1, 3 * N_SUB, D_MODEL), lambda b: (1 + b, 0, 0)),
            _const_spec((N_SUB, D_MODEL)),
            _const_spec((N_SUB, D_MODEL)),
            _const_spec((D_MODEL, IN_WIDTH)),
            _const_spec((3, CONV_WIDTH)),
            _const_spec((4, HEAD_DIM)),
            _const_spec((1, V_DIM)),
            _const_spec((D_MODEL, D_MODEL)),
            cache, cache,
            _const_spec((T, V_DIM)), _const_spec((T, V_DIM)), _const_spec((T, V_DIM)),
        ],
        out_specs=row,
        out_shape=jax.ShapeDtypeStruct((n, D_MODEL), F32),
        scratch_shapes=[
            pltpu.VMEM((T, D_MODEL), BF16),
            pltpu.VMEM((T, 3 * ATTN_WIDTH), F32),
            pltpu.VMEM((2, T, V_DIM), BF16),
            pltpu.VMEM((PAST_LEN + T, V_DIM), BF16),
            pltpu.VMEM((PAST_LEN + T, V_DIM), BF16),
            pltpu.VMEM((T, D_MODEL), BF16),
        ],
        compiler_params=pltpu.CompilerParams(vmem_limit_bytes=VMEM_LIMIT),
        name="mixer_lat",
    )(x2d, mod3, gpre, gpost, win, convw, lamqk, subg, wo, ck, cv, cos, slo, shi)


def kernel(x_prompt, x_sample, c, cache_k, cache_v, c_ctx, w_mod, b_mod, norm_pre, norm_post,
           ffn1_up, ffn1_down, ffn2_up, ffn2_down, w_in, conv_w, lam_qk, subln_g, w_o):
    batch, seq, _ = x_prompt.shape
    dec_batch, dec_seq, _ = x_sample.shape
    assert (seq, dec_seq) == (SEQ, DEC_SEQ) and cache_k.shape[1] == 1

    cond = jnp.zeros((COND_ROWS, D_MODEL), F32).at[0].set(c_ctx).at[1:1 + dec_batch].set(c)
    mod = _modulation(cond, w_mod[0], b_mod).reshape(COND_ROWS, 3 * N_SUB, D_MODEL)

    gpre, gpost = norm_pre[0], norm_post[0]
    up1, dn1 = ffn1_up[0].astype(BF16), ffn1_down[0].astype(BF16)
    up2, dn2 = ffn2_up[0].astype(BF16), ffn2_down[0].astype(BF16)
    win, wo = w_in[0].astype(BF16), w_o[0].astype(BF16)
    convw, lamqk, subg = conv_w[0], lam_qk[0], subln_g

    ctx_row = lambda t: 0
    lat_row = lambda t: 1 + t // (DEC_SEQ // FFN_TM)

    xp = x_prompt.reshape(batch * seq, D_MODEL)
    xp = _ffn(xp, mod, gpre, gpost, up1, dn1, sub=0, row_of_tile=ctx_row)
    xp, new_k, new_v = _mix_ctx(xp, mod, gpre, gpost, win, convw, lamqk, subg, wo)
    xp = _ffn(xp, mod, gpre, gpost, up2, dn2, sub=2, row_of_tile=ctx_row)

    xs = x_sample.reshape(dec_batch * dec_seq, D_MODEL)
    ck = cache_k[:, 0].reshape(dec_batch, PAST_LEN, ATTN_WIDTH)
    cv = cache_v[:, 0].reshape(dec_batch, PAST_LEN, ATTN_WIDTH)
    xs = _ffn(xs, mod, gpre, gpost, up1, dn1, sub=0, row_of_tile=lat_row)
    xs = _mix_lat(xs, mod, gpre, gpost, win, convw, lamqk, subg, wo, ck, cv)
    xs = _ffn(xs, mod, gpre, gpost, up2, dn2, sub=2, row_of_tile=lat_row)

    return (xp.reshape(batch, seq, D_MODEL), xs.reshape(dec_batch, dec_seq, D_MODEL),
            new_k.reshape(batch, 1, seq, N_HEADS, 2 * HEAD_DIM), new_v.reshape(batch, 1, seq, N_HEADS, V_DIM))
```

```python
import functools
import math

import jax
import jax.numpy as jnp
from jax import lax
from jax.experimental import pallas as pl
from jax.experimental.pallas import tpu as pltpu

D_MODEL = 1024
SEQ = 256
DEC_SEQ = 1024
PAST_LEN = 512
GRID_W = 64
ATTN_WIDTH = 512
CONV_WIDTH = 512
N_HEADS = 4
HEAD_DIM = 64
V_DIM = 128
D_FF = 2816
ROPE_BASE = 10000.0
EPS = 1e-6
N_SUB = 3
IN_WIDTH = 3 * ATTN_WIDTH + 3 * CONV_WIDTH
LAMBDA_INIT = 0.8 - 0.6 * math.exp(-0.3 * 0)
QK_SCALE = HEAD_DIM ** -0.5

F32 = jnp.float32
BF16 = jnp.bfloat16

COND_ROWS = 8
FFN_TM = 512
FFN_CHUNKS = ((0, 1024), (1024, 2048), (2048, D_FF))
CTX_TM = 512
LAT_QB = 256
VMEM_LIMIT = 60 * 1024 * 1024


def _rms(x, g):
    ms = jnp.mean(x * x, axis=-1, keepdims=True)
    return x * lax.rsqrt(ms + EPS) * g


def _dot(a, b):
    return jnp.dot(a, b, preferred_element_type=F32)


def _dot_nt(a, b):
    return lax.dot_general(a, b, (((1,), (1,)), ((), ())), preferred_element_type=F32)


def _const_spec(shape):
    nd = len(shape)
    return pl.BlockSpec(shape, lambda *_: (0,) * nd, pipeline_mode=pl.Buffered(1))


def _mod_kernel(c_ref, w_ref, b_ref, o_ref):
    c = c_ref[...]
    s = (c * jax.nn.sigmoid(c)).astype(BF16)
    o_ref[...] = _dot(s, w_ref[...].astype(BF16)) + b_ref[...]


def _modulation(cond, w_mod, b_mod):
    n_out = w_mod.shape[1]
    bn = 1024
    return pl.pallas_call(
        _mod_kernel,
        grid=(n_out // bn,),
        in_specs=[
            pl.BlockSpec((COND_ROWS, D_MODEL), lambda j: (0, 0)),
            pl.BlockSpec((D_MODEL, bn), lambda j: (0, j)),
            pl.BlockSpec((1, bn), lambda j: (0, j)),
        ],
        out_specs=pl.BlockSpec((COND_ROWS, bn), lambda j: (0, j)),
        out_shape=jax.ShapeDtypeStruct((COND_ROWS, n_out), F32),
        name="modulation",
    )(cond, w_mod, b_mod)


def _ffn_weight_copies(ci, wup_hbm, wdn_hbm, wup_ref, wdn_ref, sem):
    s, e = FFN_CHUNKS[ci]
    return (
        pltpu.make_async_copy(wup_hbm.at[:, s:e], wup_ref.at[:, s:e], sem.at[ci, 0]),
        pltpu.make_async_copy(wup_hbm.at[:, D_FF + s:D_FF + e], wup_ref.at[:, D_FF + s:D_FF + e], sem.at[ci, 1]),
        pltpu.make_async_copy(wdn_hbm.at[s:e, :], wdn_ref.at[s:e, :], sem.at[ci, 2]),
    )


def _ffn_body(x_ref, mod_ref, gpre_ref, gpost_ref, wup_ref, wdn_ref, o_ref, u_ref, acc_ref, sub, before_chunk):
    x = x_ref[...]
    sh = mod_ref[0, 3 * sub:3 * sub + 1, :]
    sc = mod_ref[0, 3 * sub + 1:3 * sub + 2, :]
    gt = mod_ref[0, 3 * sub + 2:3 * sub + 3, :]
    u = _rms(x, gpre_ref[sub:sub + 1, :]) * (1.0 + sc) + sh
    u_ref[...] = u.astype(BF16)
    for ci, (s, e) in enumerate(FFN_CHUNKS):
        before_chunk(ci)
        ub = u_ref[...]
        a = _dot(ub, wup_ref[:, s:e].astype(BF16))
        b = _dot(ub, wup_ref[:, D_FF + s:D_FF + e].astype(BF16))
        h = (a * jax.nn.sigmoid(a) * b).astype(BF16)
        part = _dot(h, wdn_ref[s:e, :].astype(BF16))
        if ci == 0:
            acc_ref[...] = part
        else:
            acc_ref[...] += part
    y = acc_ref[...]
    o_ref[...] = x + 0.5 * gt * _rms(y, gpost_ref[sub:sub + 1, :])


def _ffn_kernel(x_ref, mod_ref, gpre_ref, gpost_ref, wup_hbm, wdn_hbm, o_ref,
                wup_ref, wdn_ref, sem, u_ref, acc_ref, *, sub):
    first = pl.program_id(0) == 0
    copies = [_ffn_weight_copies(ci, wup_hbm, wdn_hbm, wup_ref, wdn_ref, sem) for ci in range(len(FFN_CHUNKS))]
    body = functools.partial(_ffn_body, x_ref, mod_ref, gpre_ref, gpost_ref, wup_ref, wdn_ref, o_ref,
                             u_ref, acc_ref, sub)

    @pl.when(first)
    def _():
        for chunk in copies:
            for cp in chunk:
                cp.start()

        def wait_chunk(ci):
            for cp in copies[ci]:
                cp.wait()

        body(wait_chunk)

    @pl.when(jnp.logical_not(first))
    def _():
        body(lambda ci: None)


def _ffn(x2d, mod3, gpre, gpost, wup, wdn, *, sub, row_of_tile):
    n = x2d.shape[0]
    tm = FFN_TM
    return pl.pallas_call(
        functools.partial(_ffn_kernel, sub=sub),
        grid=(n // tm,),
        in_specs=[
            pl.BlockSpec((tm, D_MODEL), lambda t: (t, 0)),
            pl.BlockSpec((1, 3 * N_SUB, D_MODEL), lambda t: (row_of_tile(t), 0, 0)),
            _const_spec((N_SUB, D_MODEL)),
            _const_spec((N_SUB, D_MODEL)),
            pl.BlockSpec(memory_space=pl.ANY),
            pl.BlockSpec(memory_space=pl.ANY),
        ],
        out_specs=pl.BlockSpec((tm, D_MODEL), lambda t: (t, 0)),
        out_shape=jax.ShapeDtypeStruct((n, D_MODEL), F32),
        scratch_shapes=[
            pltpu.VMEM((D_MODEL, 2 * D_FF), F32),
            pltpu.VMEM((D_FF, D_MODEL), F32),
            pltpu.SemaphoreType.DMA((len(FFN_CHUNKS), 3)),
            pltpu.VMEM((tm, D_MODEL), BF16),
            pltpu.VMEM((tm, D_MODEL), F32),
        ],
        compiler_params=pltpu.CompilerParams(dimension_semantics=("arbitrary",), vmem_limit_bytes=VMEM_LIMIT),
        name=f"ffn{sub}",
    )(x2d, mod3, gpre, gpost, wup, wdn)


def _lambda(lam_ref):
    lq = lam_ref[...]
    l1 = jnp.sum(lq[0:1, :] * lq[1:2, :], axis=-1, keepdims=True)
    l2 = jnp.sum(lq[2:3, :] * lq[3:4, :], axis=-1, keepdims=True)
    return jnp.exp(l1) - jnp.exp(l2) + LAMBDA_INIT


def _stack_maps(qh):
    lane = lax.broadcasted_iota(jnp.int32, qh.shape, 1)
    q0 = jnp.where(lane < HEAD_DIM, qh, 0.0).astype(BF16)
    q1 = jnp.where(lane >= HEAD_DIM, qh, 0.0).astype(BF16)
    return jnp.concatenate([q0, q1], axis=0)


def _diff_attn_head(qs, kh, vh, lam, subg):
    m = qs.shape[0] // 2
    s = _dot_nt(qs, kh)
    e = jnp.exp(s - jnp.max(s, axis=-1, keepdims=True))
    p = e * (1.0 / jnp.sum(e, axis=-1, keepdims=True))
    pd = (p[:m] - lam * p[m:]).astype(BF16)
    o = _dot(pd, vh)
    return _rms(o, subg) * (1.0 - LAMBDA_INIT)


def _short_conv(bg, cg, xc, convw_ref, seq):
    g = cg * xc
    m = g.shape[0]
    row = lax.broadcasted_iota(jnp.int32, g.shape, 0) % seq
    g_prev = jnp.where(row == 0, 0.0, pltpu.roll(g, 1, 0))
    g_next = jnp.where(row == seq - 1, 0.0, pltpu.roll(g, m - 1, 0))
    conv = g_prev * convw_ref[0:1, :] + g * convw_ref[1:2, :] + g_next * convw_ref[2:3, :]
    return bg * conv


def _pre_mod(x, mod_ref, gpre_ref, sub):
    sh = mod_ref[0, 3 * sub:3 * sub + 1, :]
    sc = mod_ref[0, 3 * sub + 1:3 * sub + 2, :]
    return _rms(x, gpre_ref[sub:sub + 1, :]) * (1.0 + sc) + sh


def _mix_ctx_kernel(x_ref, mod_ref, gpre_ref, gpost_ref, win_ref, convw_ref, lam_ref, subg_ref, wo_ref,
                    o_ref, k_ref, v_ref, u_ref, q_ref, cat_ref):
    A = ATTN_WIDTH
    x = x_ref[...]
    u_ref[...] = _pre_mod(x, mod_ref, gpre_ref, 1).astype(BF16)
    q_ref[...] = _dot(u_ref[...], win_ref[:, 0:A].astype(BF16)) * QK_SCALE
    k_ref[...] = _dot(u_ref[...], win_ref[:, A:2 * A].astype(BF16))
    v_ref[...] = _dot(u_ref[...], win_ref[:, 2 * A:3 * A].astype(BF16))
    lam = _lambda(lam_ref)
    subg = subg_ref[...]
    for b in range(CTX_TM // SEQ):
        rows = slice(b * SEQ, (b + 1) * SEQ)
        for h in range(N_HEADS):
            cols = slice(h * V_DIM, (h + 1) * V_DIM)
            qs = _stack_maps(q_ref[rows, cols])
            oh = _diff_attn_head(qs, k_ref[rows, cols].astype(BF16), v_ref[rows, cols].astype(BF16), lam, subg)
            cat_ref[rows, cols] = oh.astype(BF16)
    ub = u_ref[...]
    bg = _dot(ub, win_ref[:, 3 * A:3 * A + CONV_WIDTH].astype(BF16))
    cg = _dot(ub, win_ref[:, 3 * A + CONV_WIDTH:3 * A + 2 * CONV_WIDTH].astype(BF16))
    xc = _dot(ub, win_ref[:, 3 * A + 2 * CONV_WIDTH:].astype(BF16))
    cat_ref[:, A:] = _short_conv(bg, cg, xc, convw_ref, SEQ).astype(BF16)
    mix = _dot(cat_ref[...], wo_ref[...].astype(BF16))
    gt = mod_ref[0, 5:6, :]
    o_ref[...] = x + gt * _rms(mix, gpost_ref[1:2, :])


def _mix_ctx(x2d, mod3, gpre, gpost, win, convw, lamqk, subg, wo):
    n = x2d.shape[0]
    tm = CTX_TM
    row = pl.BlockSpec((tm, D_MODEL), lambda t: (t, 0))
    kv = pl.BlockSpec((tm, ATTN_WIDTH), lambda t: (t, 0))
    return pl.pallas_call(
        _mix_ctx_kernel,
        grid=(n // tm,),
        in_specs=[
            row,
            pl.BlockSpec((1, 3 * N_SUB, D_MODEL), lambda t: (0, 0, 0)),
            _const_spec((N_SUB, D_MODEL)),
            _const_spec((N_SUB, D_MODEL)),
            _const_spec((D_MODEL, IN_WIDTH)),
            _const_spec((3, CONV_WIDTH)),
            _const_spec((4, HEAD_DIM)),
            _const_spec((1, V_DIM)),
            _const_spec((D_MODEL, D_MODEL)),
        ],
        out_specs=[row, kv, kv],
        out_shape=[
            jax.ShapeDtypeStruct((n, D_MODEL), F32),
            jax.ShapeDtypeStruct((n, ATTN_WIDTH), F32),
            jax.ShapeDtypeStruct((n, ATTN_WIDTH), F32),
        ],
        scratch_shapes=[
            pltpu.VMEM((tm, D_MODEL), BF16),
            pltpu.VMEM((tm, ATTN_WIDTH), F32),
            pltpu.VMEM((tm, D_MODEL), BF16),
        ],
        compiler_params=pltpu.CompilerParams(vmem_limit_bytes=VMEM_LIMIT),
        name="mixer_ctx",
    )(x2d, mod3, gpre, gpost, win, convw, lamqk, subg, wo)


def _rope(xh, cos, sin_lo, sin_hi):
    return xh * cos + pltpu.roll(xh, 16, 1) * sin_hi + pltpu.roll(xh, V_DIM - 16, 1) * sin_lo


def _mix_lat_kernel(x_ref, mod_ref, gpre_ref, gpost_ref, win_ref, convw_ref, lam_ref, subg_ref, wo_ref,
                    ck_ref, cv_ref, cos_ref, slo_ref, shi_ref,
                    o_ref, u_ref, qkv_ref, qs_ref, kall_ref, vall_ref, cat_ref):
    A = ATTN_WIDTH
    T = DEC_SEQ
    L = PAST_LEN
    x = x_ref[...]
    u_ref[...] = _pre_mod(x, mod_ref, gpre_ref, 1).astype(BF16)
    qkv_ref[...] = _dot(u_ref[...], win_ref[:, 0:3 * A].astype(BF16))
    lam = _lambda(lam_ref)
    subg = subg_ref[...]
    cos, slo, shi = cos_ref[...], slo_ref[...], shi_ref[...]
    nqb = T // LAT_QB
    for h in range(N_HEADS):
        cols = slice(h * V_DIM, (h + 1) * V_DIM)
        qh = _rope(qkv_ref[:, h * V_DIM:(h + 1) * V_DIM], cos, slo, shi) * QK_SCALE
        lane = lax.broadcasted_iota(jnp.int32, qh.shape, 1)
        qs_ref[0] = jnp.where(lane < HEAD_DIM, qh, 0.0).astype(BF16)
        qs_ref[1] = jnp.where(lane >= HEAD_DIM, qh, 0.0).astype(BF16)
        kall_ref[0:L, :] = ck_ref[0, :, cols].astype(BF16)
        kall_ref[L:, :] = _rope(qkv_ref[:, A + h * V_DIM:A + (h + 1) * V_DIM], cos, slo, shi).astype(BF16)
        vall_ref[0:L, :] = cv_ref[0, :, cols].astype(BF16)
        vall_ref[L:, :] = qkv_ref[:, 2 * A + h * V_DIM:2 * A + (h + 1) * V_DIM].astype(BF16)

        def qblock(i, carry):
            r0 = pl.multiple_of(i * LAT_QB, LAT_QB)
            qs = jnp.concatenate([qs_ref[0, pl.ds(r0, LAT_QB), :], qs_ref[1, pl.ds(r0, LAT_QB), :]], axis=0)
            oh = _diff_attn_head(qs, kall_ref[...], vall_ref[...], lam, subg)
            cat_ref[pl.ds(r0, LAT_QB), cols] = oh.astype(BF16)
            return carry

        lax.fori_loop(0, nqb, qblock, 0)
    ub = u_ref[...]
    bg = _dot(ub, win_ref[:, 3 * A:3 * A + CONV_WIDTH].astype(BF16))
    cg = _dot(ub, win_ref[:, 3 * A + CONV_WIDTH:3 * A + 2 * CONV_WIDTH].astype(BF16))
    xc = _dot(ub, win_ref[:, 3 * A + 2 * CONV_WIDTH:].astype(BF16))
    cat_ref[:, A:] = _short_conv(bg, cg, xc, convw_ref, T).astype(BF16)
    mix = _dot(cat_ref[...], wo_ref[...].astype(BF16))
    gt = mod_ref[0, 5:6, :]
    o_ref[...] = x + gt * _rms(mix, gpost_ref[1:2, :])


def _rope_tables():
    t = jnp.arange(DEC_SEQ, dtype=jnp.int32)
    row = (t // GRID_W).astype(F32)
    col = (t % GRID_W).astype(F32)
    half = HEAD_DIM // 2
    freqs = ROPE_BASE ** (-jnp.arange(0, half, 2, dtype=F32) / half)
    ang = jnp.concatenate([row[:, None] * freqs, row[:, None] * freqs,
                           col[:, None] * freqs, col[:, None] * freqs], axis=-1)
    cos, sin = jnp.cos(ang), jnp.sin(ang)
    lane = jnp.arange(HEAD_DIM) % half
    first = (lane < half // 2)[None, :]
    sin_lo = jnp.where(first, -sin, 0.0)
    sin_hi = jnp.where(first, 0.0, sin)
    tile = lambda a: jnp.concatenate([a, a], axis=-1)
    return tile(cos), tile(sin_lo), tile(sin_hi)


def _mix_lat(x2d, mod3, gpre, gpost, win, convw, lamqk, subg, wo, ck, cv):
    n = x2d.shape[0]
    T = DEC_SEQ
    cos, slo, shi = _rope_tables()
    row = pl.BlockSpec((T, D_MODEL), lambda b: (b, 0))
    cache = pl.BlockSpec((1, PAST_LEN, ATTN_WIDTH), lambda b: (b, 0, 0))
    return pl.pallas_call(
        _mix_lat_kernel,
        grid=(n // T,),
        in_specs=[
            pl.BlockSpec((T, D_MODEL), lambda b: (b, 0), pipeline_mode=pl.Buffered(1)),
            pl.BlockSpec((1, 3 * N_SUB, D_MODEL), lambda b: (1 + b, 0, 0)),
            _const_spec((N_SUB, D_MODEL)),
            _const_spec((N_SUB, D_MODEL)),
            _const_spec((D_MODEL, IN_WIDTH)),
            _const_spec((3, CONV_WIDTH)),
            _const_spec((4, HEAD_DIM)),
            _const_spec((1, V_DIM)),
            _const_spec((D_MODEL, D_MODEL)),
            cache, cache,
            _const_spec((T, V_DIM)), _const_spec((T, V_DIM)), _const_spec((T, V_DIM)),
        ],
        out_specs=row,
        out_shape=jax.ShapeDtypeStruct((n, D_MODEL), F32),
        scratch_shapes=[
            pltpu.VMEM((T, D_MODEL), BF16),
            pltpu.VMEM((T, 3 * ATTN_WIDTH), F32),
            pltpu.VMEM((2, T, V_DIM), BF16),
            pltpu.VMEM((PAST_LEN + T, V_DIM), BF16),
            pltpu.VMEM((PAST_LEN + T, V_DIM), BF16),
            pltpu.VMEM((T, D_MODEL), BF16),
        ],
        compiler_params=pltpu.CompilerParams(vmem_limit_bytes=VMEM_LIMIT),
        name="mixer_lat",
    )(x2d, mod3, gpre, gpost, win, convw, lamqk, subg, wo, ck, cv, cos, slo, shi)


def kernel(x_prompt, x_sample, c, cache_k, cache_v, c_ctx, w_mod, b_mod, norm_pre, norm_post,
           ffn1_up, ffn1_down, ffn2_up, ffn2_down, w_in, conv_w, lam_qk, subln_g, w_o):
    batch, seq, _ = x_prompt.shape
    dec_batch, dec_seq, _ = x_sample.shape
    assert (seq, dec_seq) == (SEQ, DEC_SEQ) and cache_k.shape[1] == 1

    cond = jnp.zeros((COND_ROWS, D_MODEL), F32).at[0].set(c_ctx).at[1:1 + dec_batch].set(c)
    mod = _modulation(cond, w_mod[0], b_mod).reshape(COND_ROWS, 3 * N_SUB, D_MODEL)

    gpre, gpost = norm_pre[0], norm_post[0]
    up1, dn1, up2, dn2 = ffn1_up[0], ffn1_down[0], ffn2_up[0], ffn2_down[0]
    win, wo = w_in[0], w_o[0]
    convw, lamqk, subg = conv_w[0], lam_qk[0], subln_g

    ctx_row = lambda t: 0
    lat_row = lambda t: 1 + t // (DEC_SEQ // FFN_TM)

    xp = x_prompt.reshape(batch * seq, D_MODEL)
    xp = _ffn(xp, mod, gpre, gpost, up1, dn1, sub=0, row_of_tile=ctx_row)
    xp, new_k, new_v = _mix_ctx(xp, mod, gpre, gpost, win, convw, lamqk, subg, wo)
    xp = _ffn(xp, mod, gpre, gpost, up2, dn2, sub=2, row_of_tile=ctx_row)

    xs = x_sample.reshape(dec_batch * dec_seq, D_MODEL)
    ck = cache_k[:, 0].reshape(dec_batch, PAST_LEN, ATTN_WIDTH)
    cv = cache_v[:, 0].reshape(dec_batch, PAST_LEN, ATTN_WIDTH)
    xs = _ffn(xs, mod, gpre, gpost, up1, dn1, sub=0, row_of_tile=lat_row)
    xs = _mix_lat(xs, mod, gpre, gpost, win, convw, lamqk, subg, wo, ck, cv)
    xs = _ffn(xs, mod, gpre, gpost, up2, dn2, sub=2, row_of_tile=lat_row)

    return (xp.reshape(batch, seq, D_MODEL), xs.reshape(dec_batch, dec_seq, D_MODEL),
            new_k.reshape(batch, 1, seq, N_HEADS, 2 * HEAD_DIM), new_v.reshape(batch, 1, seq, N_HEADS, V_DIM))
```

```python
import functools
import math

import jax
import jax.numpy as jnp
from jax import lax
from jax.experimental import pallas as pl
from jax.experimental.pallas import tpu as pltpu

D_MODEL = 1024
SEQ = 256
DEC_SEQ = 1024
PAST_LEN = 512
GRID_W = 64
ATTN_WIDTH = 512
CONV_WIDTH = 512
N_HEADS = 4
HEAD_DIM = 64
V_DIM = 128
D_FF = 2816
ROPE_BASE = 10000.0
EPS = 1e-6
N_SUB = 3
IN_WIDTH = 3 * ATTN_WIDTH + 3 * CONV_WIDTH
LAMBDA_INIT = 0.8 - 0.6 * math.exp(-0.3 * 0)
QK_SCALE_LOG2 = HEAD_DIM ** -0.5 * math.log2(math.e)

F32 = jnp.float32
BF16 = jnp.bfloat16

MXU_TILE = 256
COND_ROWS = 8
FFN_TM = 512
FFN_CHUNKS = ((0, 1024), (1024, 2048), (2048, D_FF))
CTX_TM = 512
LAT_QB = 256
VMEM_LIMIT = 60 * 1024 * 1024


def _rms(x, g):
    ms = jnp.mean(x * x, axis=-1, keepdims=True)
    return x * lax.rsqrt(ms + EPS) * g


def _dot(a, b):
    return jnp.dot(a, b, preferred_element_type=F32)


def _dot_nt(a, b):
    return lax.dot_general(a, b, (((1,), (1,)), ((), ())), preferred_element_type=F32)


def _const_spec(shape):
    nd = len(shape)
    return pl.BlockSpec(shape, lambda *_: (0,) * nd, pipeline_mode=pl.Buffered(1))


def _mod_kernel(c_ref, w_ref, b_ref, o_ref):
    c = c_ref[...]
    s = (c * jax.nn.sigmoid(c)).astype(BF16)
    o_ref[...] = _dot(s, w_ref[...].astype(BF16)) + b_ref[...]


def _modulation(cond, w_mod, b_mod):
    n_out = w_mod.shape[1]
    bn = 1024
    return pl.pallas_call(
        _mod_kernel,
        grid=(n_out // bn,),
        in_specs=[
            pl.BlockSpec((COND_ROWS, D_MODEL), lambda j: (0, 0)),
            pl.BlockSpec((D_MODEL, bn), lambda j: (0, j)),
            pl.BlockSpec((1, bn), lambda j: (0, j)),
        ],
        out_specs=pl.BlockSpec((COND_ROWS, bn), lambda j: (0, j)),
        out_shape=jax.ShapeDtypeStruct((COND_ROWS, n_out), F32),
        name="modulation",
    )(cond, w_mod, b_mod)


def _ffn_weight_copies(ci, wup_hbm, wdn_hbm, wup_ref, wdn_ref, sem):
    s, e = FFN_CHUNKS[ci]
    return (
        pltpu.make_async_copy(wup_hbm.at[:, s:e], wup_ref.at[:, s:e], sem.at[ci, 0]),
        pltpu.make_async_copy(wup_hbm.at[:, D_FF + s:D_FF + e], wup_ref.at[:, D_FF + s:D_FF + e], sem.at[ci, 1]),
        pltpu.make_async_copy(wdn_hbm.at[s:e, :], wdn_ref.at[s:e, :], sem.at[ci, 2]),
    )


def _ffn_body(x_ref, mod_ref, gpre_ref, gpost_ref, wup_ref, wdn_ref, o_ref, u_ref, acc_ref, sub, before_chunk):
    x = x_ref[...]
    sh = mod_ref[0, 3 * sub:3 * sub + 1, :]
    sc = mod_ref[0, 3 * sub + 1:3 * sub + 2, :]
    gt = mod_ref[0, 3 * sub + 2:3 * sub + 3, :]
    u = _rms(x, gpre_ref[sub:sub + 1, :]) * (1.0 + sc) + sh
    u_ref[...] = u.astype(BF16)
    for ci, (s, e) in enumerate(FFN_CHUNKS):
        before_chunk(ci)
        ub = u_ref[...]
        a = _dot(ub, wup_ref[:, s:e].astype(BF16))
        b = _dot(ub, wup_ref[:, D_FF + s:D_FF + e].astype(BF16))
        h = (a * jax.nn.sigmoid(a) * b).astype(BF16)
        part = _dot(h, wdn_ref[s:e, :].astype(BF16))
        if ci == 0:
            acc_ref[...] = part
        else:
            acc_ref[...] += part
    y = acc_ref[...]
    o_ref[...] = x + 0.5 * gt * _rms(y, gpost_ref[sub:sub + 1, :])


def _ffn_kernel(x_ref, mod_ref, gpre_ref, gpost_ref, wup_hbm, wdn_hbm, o_ref,
                wup_ref, wdn_ref, sem, u_ref, acc_ref, *, sub):
    first = pl.program_id(0) == 0
    copies = [_ffn_weight_copies(ci, wup_hbm, wdn_hbm, wup_ref, wdn_ref, sem) for ci in range(len(FFN_CHUNKS))]
    body = functools.partial(_ffn_body, x_ref, mod_ref, gpre_ref, gpost_ref, wup_ref, wdn_ref, o_ref,
                             u_ref, acc_ref, sub)

    def start_chunk(ci):
        for cp in copies[ci]:
            cp.start()

    @pl.when(first)
    def _():
        start_chunk(0)

        def wait_chunk(ci):
            for cp in copies[ci]:
                cp.wait()
            if ci + 1 < len(copies):
                start_chunk(ci + 1)

        body(wait_chunk)

    @pl.when(jnp.logical_not(first))
    def _():
        body(lambda ci: None)


def _ffn(x2d, mod3, gpre, gpost, wup, wdn, *, sub, row_of_tile):
    n = x2d.shape[0]
    tm = FFN_TM
    return pl.pallas_call(
        functools.partial(_ffn_kernel, sub=sub),
        grid=(n // tm,),
        in_specs=[
            pl.BlockSpec((tm, D_MODEL), lambda t: (t, 0)),
            pl.BlockSpec((1, 3 * N_SUB, D_MODEL), lambda t: (row_of_tile(t), 0, 0)),
            _const_spec((N_SUB, D_MODEL)),
            _const_spec((N_SUB, D_MODEL)),
            pl.BlockSpec(memory_space=pl.ANY),
            pl.BlockSpec(memory_space=pl.ANY),
        ],
        out_specs=pl.BlockSpec((tm, D_MODEL), lambda t: (t, 0)),
        out_shape=jax.ShapeDtypeStruct((n, D_MODEL), F32),
        scratch_shapes=[
            pltpu.VMEM((D_MODEL, 2 * D_FF), F32),
            pltpu.VMEM((D_FF, D_MODEL), F32),
            pltpu.SemaphoreType.DMA((len(FFN_CHUNKS), 3)),
            pltpu.VMEM((tm, D_MODEL), BF16),
            pltpu.VMEM((tm, D_MODEL), F32),
        ],
        compiler_params=pltpu.CompilerParams(dimension_semantics=("arbitrary",), vmem_limit_bytes=VMEM_LIMIT),
        name=f"ffn{sub}",
    )(x2d, mod3, gpre, gpost, wup, wdn)


def _lambda(lam_ref):
    lq = lam_ref[...]
    l1 = jnp.sum(lq[0:1, :] * lq[1:2, :], axis=-1, keepdims=True)
    l2 = jnp.sum(lq[2:3, :] * lq[3:4, :], axis=-1, keepdims=True)
    return jnp.exp(l1) - jnp.exp(l2) + LAMBDA_INIT


def _split_maps(qh):
    lane = lax.broadcasted_iota(jnp.int32, qh.shape, 1)
    q0 = jnp.where(lane < HEAD_DIM, qh, 0.0).astype(BF16)
    q1 = jnp.where(lane >= HEAD_DIM, qh, 0.0).astype(BF16)
    return q0, q1


def _attn_scores(s_ref, idx, qs, kh):
    s_ref[idx] = _dot_nt(qs, kh)


def _attn_probs(s_ref, pd_ref, rz_ref, idx, lam):
    s = s_ref[idx]
    m = s.shape[0] // 2
    e = jnp.exp2(s - jnp.max(s, axis=-1, keepdims=True))
    z = jnp.sum(e, axis=-1, keepdims=True)
    c = lam * z[:m] / z[m:]
    pd_ref[idx] = (e[:m] - c * e[m:]).astype(BF16)
    rz_ref[idx] = jnp.broadcast_to(1.0 / z[:m], (m, V_DIM))


def _attn_values(pd_ref, rz_ref, idx, vh, subg):
    o = _dot(pd_ref[idx], vh) * rz_ref[idx]
    return (_rms(o, subg) * (1.0 - LAMBDA_INIT)).astype(BF16)


def _short_conv(bg, cg, xc, convw_ref, seq):
    g = cg * xc
    m = g.shape[0]
    row = lax.broadcasted_iota(jnp.int32, g.shape, 0) % seq
    g_prev = jnp.where(row == 0, 0.0, pltpu.roll(g, 1, 0))
    g_next = jnp.where(row == seq - 1, 0.0, pltpu.roll(g, m - 1, 0))
    conv = g_prev * convw_ref[0:1, :] + g * convw_ref[1:2, :] + g_next * convw_ref[2:3, :]
    return bg * conv


def _pre_mod(x, mod_ref, gpre_ref, sub):
    sh = mod_ref[0, 3 * sub:3 * sub + 1, :]
    sc = mod_ref[0, 3 * sub + 1:3 * sub + 2, :]
    return _rms(x, gpre_ref[sub:sub + 1, :]) * (1.0 + sc) + sh


def _mix_ctx_kernel(x_ref, mod_ref, gpre_ref, gpost_ref, win_ref, convw_ref, lam_ref, subg_ref, wo_ref,
                    o_ref, k_ref, v_ref, u_ref, q_ref, cat_ref, s_ref, pd_ref, rz_ref, cv_ref, mixc_ref):
    A = ATTN_WIDTH
    x = x_ref[...]
    u_ref[...] = _pre_mod(x, mod_ref, gpre_ref, 1).astype(BF16)
    q_ref[...] = _dot(u_ref[...], win_ref[:, 0:A].astype(BF16)) * QK_SCALE_LOG2
    k_ref[...] = _dot(u_ref[...], win_ref[:, A:2 * A].astype(BF16))
    v_ref[...] = _dot(u_ref[...], win_ref[:, 2 * A:3 * A].astype(BF16))
    lam = _lambda(lam_ref)
    subg = subg_ref[...]
    n_units = (CTX_TM // SEQ) * N_HEADS
    n_conv_pieces = 3 * CONV_WIDTH // MXU_TILE

    def unit_slices(uid):
        b, h = divmod(uid, N_HEADS)
        return slice(b * SEQ, (b + 1) * SEQ), slice(h * V_DIM, (h + 1) * V_DIM)

    def conv_piece(i):
        c0 = 3 * A + i * MXU_TILE
        cv_ref[:, i * MXU_TILE:(i + 1) * MXU_TILE] = _dot(u_ref[...], win_ref[:, c0:c0 + MXU_TILE].astype(BF16))

    def mix_conv_piece(j):
        cols = slice(j * MXU_TILE, (j + 1) * MXU_TILE)
        mixc_ref[:, cols] = _dot(cat_ref[:, A:], wo_ref[A:, cols].astype(BF16))

    for uid in range(n_units):
        rows, cols = unit_slices(uid)
        q0, q1 = _split_maps(q_ref[rows, cols])
        _attn_scores(s_ref, uid, jnp.concatenate([q0, q1], axis=0), k_ref[rows, cols].astype(BF16))
    for uid in range(n_units):
        _attn_probs(s_ref, pd_ref, rz_ref, uid, lam)
        if uid < n_conv_pieces:
            conv_piece(uid)
    bg = cv_ref[:, 0:CONV_WIDTH]
    cg = cv_ref[:, CONV_WIDTH:2 * CONV_WIDTH]
    xc = cv_ref[:, 2 * CONV_WIDTH:]
    cat_ref[:, A:] = _short_conv(bg, cg, xc, convw_ref, SEQ).astype(BF16)
    for uid in range(n_units):
        rows, cols = unit_slices(uid)
        cat_ref[rows, cols] = _attn_values(pd_ref, rz_ref, uid, v_ref[rows, cols].astype(BF16), subg)
        if uid % 2 == 1:
            mix_conv_piece(uid // 2)
    mix = mixc_ref[...] + _dot(cat_ref[:, :A], wo_ref[:A, :].astype(BF16))
    gt = mod_ref[0, 5:6, :]
    o_ref[...] = x + gt * _rms(mix, gpost_ref[1:2, :])


def _mix_ctx(x2d, mod3, gpre, gpost, win, convw, lamqk, subg, wo):
    n = x2d.shape[0]
    tm = CTX_TM
    n_units = (tm // SEQ) * N_HEADS
    assert 3 * CONV_WIDTH // MXU_TILE <= n_units and D_MODEL // MXU_TILE == n_units // 2
    row = pl.BlockSpec((tm, D_MODEL), lambda t: (t, 0))
    kv = pl.BlockSpec((tm, ATTN_WIDTH), lambda t: (t, 0))
    return pl.pallas_call(
        _mix_ctx_kernel,
        grid=(n // tm,),
        in_specs=[
            row,
            pl.BlockSpec((1, 3 * N_SUB, D_MODEL), lambda t: (0, 0, 0)),
            _const_spec((N_SUB, D_MODEL)),
            _const_spec((N_SUB, D_MODEL)),
            _const_spec((D_MODEL, IN_WIDTH)),
            _const_spec((3, CONV_WIDTH)),
            _const_spec((4, HEAD_DIM)),
            _const_spec((1, V_DIM)),
            _const_spec((D_MODEL, D_MODEL)),
        ],
        out_specs=[row, kv, kv],
        out_shape=[
            jax.ShapeDtypeStruct((n, D_MODEL), F32),
            jax.ShapeDtypeStruct((n, ATTN_WIDTH), F32),
            jax.ShapeDtypeStruct((n, ATTN_WIDTH), F32),
        ],
        scratch_shapes=[
            pltpu.VMEM((tm, D_MODEL), BF16),
            pltpu.VMEM((tm, ATTN_WIDTH), F32),
            pltpu.VMEM((tm, D_MODEL), BF16),
            pltpu.VMEM((n_units, 2 * SEQ, SEQ), F32),
            pltpu.VMEM((n_units, SEQ, SEQ), BF16),
            pltpu.VMEM((n_units, SEQ, V_DIM), F32),
            pltpu.VMEM((tm, 3 * CONV_WIDTH), F32),
            pltpu.VMEM((tm, D_MODEL), F32),
        ],
        compiler_params=pltpu.CompilerParams(vmem_limit_bytes=VMEM_LIMIT),
        name="mixer_ctx",
    )(x2d, mod3, gpre, gpost, win, convw, lamqk, subg, wo)


def _rope(xh, cos, sin_lo, sin_hi):
    return xh * cos + pltpu.roll(xh, 16, 1) * sin_hi + pltpu.roll(xh, V_DIM - 16, 1) * sin_lo


def _rope_tables():
    t = jnp.arange(DEC_SEQ, dtype=jnp.int32)
    row = (t // GRID_W).astype(F32)
    col = (t % GRID_W).astype(F32)
    half = HEAD_DIM // 2
    freqs = ROPE_BASE ** (-jnp.arange(0, half, 2, dtype=F32) / half)
    ang = jnp.concatenate([row[:, None] * freqs, row[:, None] * freqs,
                           col[:, None] * freqs, col[:, None] * freqs], axis=-1)
    cos, sin = jnp.cos(ang), jnp.sin(ang)
    lane = jnp.arange(HEAD_DIM) % half
    first = (lane < half // 2)[None, :]
    sin_lo = jnp.where(first, -sin, 0.0)
    sin_hi = jnp.where(first, 0.0, sin)
    tile = lambda a: jnp.concatenate([a, a], axis=-1)
    return tile(cos), tile(sin_lo), tile(sin_hi)


def _lat_proj_kernel(x_ref, mod_ref, gpre_ref, win_ref, convw_ref, ck_ref, cv_ref, cos_ref, slo_ref, shi_ref,
                     qs_ref, kall_ref, vall_ref, conv_ref, u_ref):
    A, T, L = ATTN_WIDTH, DEC_SEQ, PAST_LEN
    u_ref[...] = _pre_mod(x_ref[...], mod_ref, gpre_ref, 1).astype(BF16)
    cos, slo, shi = cos_ref[...], slo_ref[...], shi_ref[...]
    q = _dot(u_ref[...], win_ref[:, 0:A].astype(BF16))
    for h in range(N_HEADS):
        cols = slice(h * V_DIM, (h + 1) * V_DIM)
        q0, q1 = _split_maps(_rope(q[:, cols], cos, slo, shi) * QK_SCALE_LOG2)
        for qb in range(T // LAT_QB):
            rows = slice(qb * LAT_QB, (qb + 1) * LAT_QB)
            qs_ref[0, h, qb, 0:LAT_QB, :] = q0[rows]
            qs_ref[0, h, qb, LAT_QB:, :] = q1[rows]
    k = _dot(u_ref[...], win_ref[:, A:2 * A].astype(BF16))
    for h in range(N_HEADS):
        cols = slice(h * V_DIM, (h + 1) * V_DIM)
        kall_ref[0, h, 0:L, :] = ck_ref[0, :, cols].astype(BF16)
        kall_ref[0, h, L:, :] = _rope(k[:, cols], cos, slo, shi).astype(BF16)
    v = _dot(u_ref[...], win_ref[:, 2 * A:3 * A].astype(BF16))
    for h in range(N_HEADS):
        cols = slice(h * V_DIM, (h + 1) * V_DIM)
        vall_ref[0, h, 0:L, :] = cv_ref[0, :, cols].astype(BF16)
        vall_ref[0, h, L:, :] = v[:, cols].astype(BF16)
    ub = u_ref[...]
    bg = _dot(ub, win_ref[:, 3 * A:3 * A + CONV_WIDTH].astype(BF16))
    cg = _dot(ub, win_ref[:, 3 * A + CONV_WIDTH:3 * A + 2 * CONV_WIDTH].astype(BF16))
    xc = _dot(ub, win_ref[:, 3 * A + 2 * CONV_WIDTH:].astype(BF16))
    conv_ref[...] = _short_conv(bg, cg, xc, convw_ref, T).astype(BF16)


def _lat_proj(x2d, mod3, gpre, win, convw, ck, cv):
    n = x2d.shape[0]
    T, L = DEC_SEQ, PAST_LEN
    nb = n // T
    nqb = T // LAT_QB
    cos, slo, shi = _rope_tables()
    cache = pl.BlockSpec((1, L, ATTN_WIDTH), lambda b: (b, 0, 0))
    return pl.pallas_call(
        _lat_proj_kernel,
        grid=(nb,),
        in_specs=[
            pl.BlockSpec((T, D_MODEL), lambda b: (b, 0)),
            pl.BlockSpec((1, 3 * N_SUB, D_MODEL), lambda b: (1 + b, 0, 0)),
            _const_spec((N_SUB, D_MODEL)),
            _const_spec((D_MODEL, IN_WIDTH)),
            _const_spec((3, CONV_WIDTH)),
            cache, cache,
            _const_spec((T, V_DIM)), _const_spec((T, V_DIM)), _const_spec((T, V_DIM)),
        ],
        out_specs=[
            pl.BlockSpec((1, N_HEADS, nqb, 2 * LAT_QB, V_DIM), lambda b: (b, 0, 0, 0, 0)),
            pl.BlockSpec((1, N_HEADS, L + T, V_DIM), lambda b: (b, 0, 0, 0)),
            pl.BlockSpec((1, N_HEADS, L + T, V_DIM), lambda b: (b, 0, 0, 0)),
            pl.BlockSpec((T, CONV_WIDTH), lambda b: (b, 0)),
        ],
        out_shape=[
            jax.ShapeDtypeStruct((nb, N_HEADS, nqb, 2 * LAT_QB, V_DIM), BF16),
            jax.ShapeDtypeStruct((nb, N_HEADS, L + T, V_DIM), BF16),
            jax.ShapeDtypeStruct((nb, N_HEADS, L + T, V_DIM), BF16),
            jax.ShapeDtypeStruct((n, CONV_WIDTH), BF16),
        ],
        scratch_shapes=[pltpu.VMEM((T, D_MODEL), BF16)],
        compiler_params=pltpu.CompilerParams(vmem_limit_bytes=VMEM_LIMIT),
        name="mixer_lat_proj",
    )(x2d, mod3, gpre, win, convw, ck, cv, cos, slo, shi)


def _lat_attn_kernel(x_ref, mod_ref, gpost_ref, qs_ref, kall_ref, vall_ref, conv_ref, lam_ref, subg_ref, wo_ref,
                     o_ref, cat_ref, s_ref, pd_ref, rz_ref):
    A = ATTN_WIDTH
    lam = _lambda(lam_ref)
    subg = subg_ref[...]

    def scores(h):
        _attn_scores(s_ref, h, qs_ref[0, h, 0], kall_ref[0, h])

    def probs(h):
        _attn_probs(s_ref, pd_ref, rz_ref, h, lam)

    def values(h):
        cat_ref[:, h * V_DIM:(h + 1) * V_DIM] = _attn_values(pd_ref, rz_ref, h, vall_ref[0, h], subg)

    scores(0)
    scores(1)
    mix_conv = _dot(conv_ref[...], wo_ref[A:, :].astype(BF16))
    for h in range(N_HEADS):
        probs(h)
        if h + 2 < N_HEADS:
            scores(h + 2)
        if h >= 1:
            values(h - 1)
    values(N_HEADS - 1)
    mix = mix_conv + _dot(cat_ref[...], wo_ref[:A, :].astype(BF16))
    gt = mod_ref[0, 5:6, :]
    o_ref[...] = x_ref[...] + gt * _rms(mix, gpost_ref[1:2, :])


def _lat_attn(x2d, mod3, gpost, qs, kall, vall, conv, lamqk, subg, wo):
    n = x2d.shape[0]
    T, L = DEC_SEQ, PAST_LEN
    nb = n // T
    nqb = T // LAT_QB
    row = pl.BlockSpec((LAT_QB, D_MODEL), lambda b, i: (b * nqb + i, 0))
    keys = pl.BlockSpec((1, N_HEADS, L + T, V_DIM), lambda b, i: (b, 0, 0, 0))
    return pl.pallas_call(
        _lat_attn_kernel,
        grid=(nb, nqb),
        in_specs=[
            row,
            pl.BlockSpec((1, 3 * N_SUB, D_MODEL), lambda b, i: (1 + b, 0, 0)),
            _const_spec((N_SUB, D_MODEL)),
            pl.BlockSpec((1, N_HEADS, 1, 2 * LAT_QB, V_DIM), lambda b, i: (b, 0, i, 0, 0)),
            keys, keys,
            pl.BlockSpec((LAT_QB, CONV_WIDTH), lambda b, i: (b * nqb + i, 0)),
            _const_spec((4, HEAD_DIM)),
            _const_spec((1, V_DIM)),
            _const_spec((D_MODEL, D_MODEL)),
        ],
        out_specs=row,
        out_shape=jax.ShapeDtypeStruct((n, D_MODEL), F32),
        scratch_shapes=[
            pltpu.VMEM((LAT_QB, ATTN_WIDTH), BF16),
            pltpu.VMEM((N_HEADS, 2 * LAT_QB, L + T), F32),
            pltpu.VMEM((N_HEADS, LAT_QB, L + T), BF16),
            pltpu.VMEM((N_HEADS, LAT_QB, V_DIM), F32),
        ],
        compiler_params=pltpu.CompilerParams(vmem_limit_bytes=VMEM_LIMIT),
        name="mixer_lat_attn",
    )(x2d, mod3, gpost, qs, kall, vall, conv, lamqk, subg, wo)


def kernel(x_prompt, x_sample, c, cache_k, cache_v, c_ctx, w_mod, b_mod, norm_pre, norm_post,
           ffn1_up, ffn1_down, ffn2_up, ffn2_down, w_in, conv_w, lam_qk, subln_g, w_o):
    batch, seq, _ = x_prompt.shape
    dec_batch, dec_seq, _ = x_sample.shape
    assert (seq, dec_seq) == (SEQ, DEC_SEQ) and cache_k.shape[1] == 1

    cond = jnp.zeros((COND_ROWS, D_MODEL), F32).at[0].set(c_ctx).at[1:1 + dec_batch].set(c)
    mod = _modulation(cond, w_mod[0], b_mod).reshape(COND_ROWS, 3 * N_SUB, D_MODEL)

    gpre, gpost = norm_pre[0], norm_post[0]
    up1, dn1, up2, dn2 = ffn1_up[0], ffn1_down[0], ffn2_up[0], ffn2_down[0]
    win, wo = w_in[0], w_o[0]
    convw, lamqk, subg = conv_w[0], lam_qk[0], subln_g

    ctx_row = lambda t: 0
    lat_row = lambda t: 1 + t // (DEC_SEQ // FFN_TM)

    xp = x_prompt.reshape(batch * seq, D_MODEL)
    xp = _ffn(xp, mod, gpre, gpost, up1, dn1, sub=0, row_of_tile=ctx_row)
    xp, new_k, new_v = _mix_ctx(xp, mod, gpre, gpost, win, convw, lamqk, subg, wo)
    xp = _ffn(xp, mod, gpre, gpost, up2, dn2, sub=2, row_of_tile=ctx_row)

    xs = x_sample.reshape(dec_batch * dec_seq, D_MODEL)
    ck = cache_k[:, 0].reshape(dec_batch, PAST_LEN, ATTN_WIDTH)
    cv = cache_v[:, 0].reshape(dec_batch, PAST_LEN, ATTN_WIDTH)
    xs = _ffn(xs, mod, gpre, gpost, up1, dn1, sub=0, row_of_tile=lat_row)
    qs, kall, vall, conv = _lat_proj(xs, mod, gpre, win, convw, ck, cv)
    xs = _lat_attn(xs, mod, gpost, qs, kall, vall, conv, lamqk, subg, wo)
    xs = _ffn(xs, mod, gpre, gpost, up2, dn2, sub=2, row_of_tile=lat_row)

    return (xp.reshape(batch, seq, D_MODEL), xs.reshape(dec_batch, dec_seq, D_MODEL),
            new_k.reshape(batch, 1, seq, N_HEADS, 2 * HEAD_DIM), new_v.reshape(batch, 1, seq, N_HEADS, V_DIM))
```

```python
import functools
import math

import jax
import jax.numpy as jnp
from jax import lax
from jax.experimental import pallas as pl
from jax.experimental.pallas import tpu as pltpu

D_MODEL = 1024
SEQ = 256
DEC_SEQ = 1024
PAST_LEN = 512
GRID_W = 64
ATTN_WIDTH = 512
CONV_WIDTH = 512
N_HEADS = 4
HEAD_DIM = 64
V_DIM = 128
D_FF = 2816
ROPE_BASE = 10000.0
EPS = 1e-6
N_SUB = 3
IN_WIDTH = 3 * ATTN_WIDTH + 3 * CONV_WIDTH
LAMBDA_INIT = 0.8 - 0.6 * math.exp(-0.3 * 0)
QK_SCALE_LOG2 = HEAD_DIM ** -0.5 * math.log2(math.e)

F32 = jnp.float32
BF16 = jnp.bfloat16

MXU_TILE = 256
COND_ROWS = 8
FFN_TM = 512
FFN_CHUNKS = ((0, 1024), (1024, 2048), (2048, D_FF))
FFN_DMA_SPLIT = 4
CTX_TM = 512
LAT_QB = 256
VMEM_LIMIT = 60 * 1024 * 1024


def _rms(x, g):
    ms = jnp.mean(x * x, axis=-1, keepdims=True)
    return x * lax.rsqrt(ms + EPS) * g


def _dot(a, b):
    return jnp.dot(a, b, preferred_element_type=F32)


def _dot_nt(a, b):
    return lax.dot_general(a, b, (((1,), (1,)), ((), ())), preferred_element_type=F32)


def _const_spec(shape):
    nd = len(shape)
    return pl.BlockSpec(shape, lambda *_: (0,) * nd, pipeline_mode=pl.Buffered(1))


def _mod_kernel(c_ref, w_ref, b_ref, o_ref):
    c = c_ref[...]
    s = (c * jax.nn.sigmoid(c)).astype(BF16)
    o_ref[...] = _dot(s, w_ref[...].astype(BF16)) + b_ref[...]


def _modulation(cond, w_mod, b_mod):
    n_out = w_mod.shape[1]
    bn = 1024
    return pl.pallas_call(
        _mod_kernel,
        grid=(n_out // bn,),
        in_specs=[
            pl.BlockSpec((COND_ROWS, D_MODEL), lambda j: (0, 0)),
            pl.BlockSpec((D_MODEL, bn), lambda j: (0, j)),
            pl.BlockSpec((1, bn), lambda j: (0, j)),
        ],
        out_specs=pl.BlockSpec((COND_ROWS, bn), lambda j: (0, j)),
        out_shape=jax.ShapeDtypeStruct((COND_ROWS, n_out), F32),
        name="modulation",
    )(cond, w_mod, b_mod)


def _ffn_weight_copies(ci, wup_hbm, wdn_hbm, wup_ref, wdn_ref, sem):
    s, e = FFN_CHUNKS[ci]
    copies = []
    for j in range(FFN_DMA_SPLIT):
        r = slice(j * D_MODEL // FFN_DMA_SPLIT, (j + 1) * D_MODEL // FFN_DMA_SPLIT)
        d = slice(s + j * (e - s) // FFN_DMA_SPLIT, s + (j + 1) * (e - s) // FFN_DMA_SPLIT)
        copies += [
            pltpu.make_async_copy(wup_hbm.at[r, s:e], wup_ref.at[r, s:e], sem.at[ci, 3 * j]),
            pltpu.make_async_copy(wup_hbm.at[r, D_FF + s:D_FF + e], wup_ref.at[r, D_FF + s:D_FF + e],
                                  sem.at[ci, 3 * j + 1]),
            pltpu.make_async_copy(wdn_hbm.at[d, :], wdn_ref.at[d, :], sem.at[ci, 3 * j + 2]),
        ]
    return copies


def _ffn_body(x_ref, mod_ref, gpre_ref, gpost_ref, wup_ref, wdn_ref, o_ref, u_ref, acc_ref, sub, before_chunk):
    x = x_ref[...]
    sh = mod_ref[0, 3 * sub:3 * sub + 1, :]
    sc = mod_ref[0, 3 * sub + 1:3 * sub + 2, :]
    gt = mod_ref[0, 3 * sub + 2:3 * sub + 3, :]
    u = _rms(x, gpre_ref[sub:sub + 1, :]) * (1.0 + sc) + sh
    u_ref[...] = u.astype(BF16)
    for ci, (s, e) in enumerate(FFN_CHUNKS):
        before_chunk(ci)
        ub = u_ref[...]
        a = _dot(ub, wup_ref[:, s:e].astype(BF16))
        b = _dot(ub, wup_ref[:, D_FF + s:D_FF + e].astype(BF16))
        h = (a * jax.nn.sigmoid(a) * b).astype(BF16)
        part = _dot(h, wdn_ref[s:e, :].astype(BF16))
        if ci == 0:
            acc_ref[...] = part
        else:
            acc_ref[...] += part
    y = acc_ref[...]
    o_ref[...] = x + 0.5 * gt * _rms(y, gpost_ref[sub:sub + 1, :])


def _ffn_kernel(x_ref, mod_ref, gpre_ref, gpost_ref, wup_hbm, wdn_hbm, o_ref,
                wup_ref, wdn_ref, sem, u_ref, acc_ref, *, sub):
    first = pl.program_id(0) == 0
    copies = [_ffn_weight_copies(ci, wup_hbm, wdn_hbm, wup_ref, wdn_ref, sem) for ci in range(len(FFN_CHUNKS))]
    body = functools.partial(_ffn_body, x_ref, mod_ref, gpre_ref, gpost_ref, wup_ref, wdn_ref, o_ref,
                             u_ref, acc_ref, sub)

    def start_chunk(ci):
        for i, cp in enumerate(copies[ci]):
            cp.start(priority=i % 2)

    @pl.when(first)
    def _():
        start_chunk(0)

        def wait_chunk(ci):
            for cp in copies[ci]:
                cp.wait()
            if ci + 1 < len(copies):
                start_chunk(ci + 1)

        body(wait_chunk)

    @pl.when(jnp.logical_not(first))
    def _():
        body(lambda ci: None)


def _ffn(x2d, mod3, gpre, gpost, wup, wdn, *, sub, row_of_tile):
    n = x2d.shape[0]
    tm = FFN_TM
    return pl.pallas_call(
        functools.partial(_ffn_kernel, sub=sub),
        grid=(n // tm,),
        in_specs=[
            pl.BlockSpec((tm, D_MODEL), lambda t: (t, 0)),
            pl.BlockSpec((1, 3 * N_SUB, D_MODEL), lambda t: (row_of_tile(t), 0, 0)),
            _const_spec((N_SUB, D_MODEL)),
            _const_spec((N_SUB, D_MODEL)),
            pl.BlockSpec(memory_space=pl.ANY),
            pl.BlockSpec(memory_space=pl.ANY),
        ],
        out_specs=pl.BlockSpec((tm, D_MODEL), lambda t: (t, 0)),
        out_shape=jax.ShapeDtypeStruct((n, D_MODEL), F32),
        scratch_shapes=[
            pltpu.VMEM((D_MODEL, 2 * D_FF), F32),
            pltpu.VMEM((D_FF, D_MODEL), F32),
            pltpu.SemaphoreType.DMA((len(FFN_CHUNKS), 3 * FFN_DMA_SPLIT)),
            pltpu.VMEM((tm, D_MODEL), BF16),
            pltpu.VMEM((tm, D_MODEL), F32),
        ],
        compiler_params=pltpu.CompilerParams(dimension_semantics=("arbitrary",), vmem_limit_bytes=VMEM_LIMIT),
        name=f"ffn{sub}",
    )(x2d, mod3, gpre, gpost, wup, wdn)


def _lambda(lam_ref):
    lq = lam_ref[...]
    l1 = jnp.sum(lq[0:1, :] * lq[1:2, :], axis=-1, keepdims=True)
    l2 = jnp.sum(lq[2:3, :] * lq[3:4, :], axis=-1, keepdims=True)
    return jnp.exp(l1) - jnp.exp(l2) + LAMBDA_INIT


def _split_maps(qh):
    lane = lax.broadcasted_iota(jnp.int32, qh.shape, 1)
    q0 = jnp.where(lane < HEAD_DIM, qh, 0.0).astype(BF16)
    q1 = jnp.where(lane >= HEAD_DIM, qh, 0.0).astype(BF16)
    return q0, q1


def _attn_scores(s_ref, idx, qs, kh):
    s_ref[idx] = _dot_nt(qs, kh)


def _attn_probs(s_ref, pd_ref, rz_ref, idx, lam):
    s = s_ref[idx]
    m = s.shape[0] // 2
    e = jnp.exp2(s - jnp.max(s, axis=-1, keepdims=True))
    z = jnp.sum(e, axis=-1, keepdims=True)
    c = lam * z[:m] / z[m:]
    pd_ref[idx] = (e[:m] - c * e[m:]).astype(BF16)
    rz_ref[idx] = jnp.broadcast_to(1.0 / z[:m], (m, V_DIM))


def _attn_values(pd_ref, rz_ref, idx, vh, subg):
    o = _dot(pd_ref[idx], vh) * rz_ref[idx]
    return (_rms(o, subg) * (1.0 - LAMBDA_INIT)).astype(BF16)


def _short_conv(bg, cg, xc, convw_ref, seq):
    g = cg * xc
    m = g.shape[0]
    row = lax.broadcasted_iota(jnp.int32, g.shape, 0) % seq
    g_prev = jnp.where(row == 0, 0.0, pltpu.roll(g, 1, 0))
    g_next = jnp.where(row == seq - 1, 0.0, pltpu.roll(g, m - 1, 0))
    conv = g_prev * convw_ref[0:1, :] + g * convw_ref[1:2, :] + g_next * convw_ref[2:3, :]
    return bg * conv


def _pre_mod(x, mod_ref, gpre_ref, sub):
    sh = mod_ref[0, 3 * sub:3 * sub + 1, :]
    sc = mod_ref[0, 3 * sub + 1:3 * sub + 2, :]
    return _rms(x, gpre_ref[sub:sub + 1, :]) * (1.0 + sc) + sh


def _mix_ctx_kernel(x_ref, mod_ref, gpre_ref, gpost_ref, win_ref, convw_ref, lam_ref, subg_ref, wo_ref,
                    o_ref, k_ref, v_ref, u_ref, q_ref, kb_ref, vb_ref, cat_ref, s_ref, pd_ref, rz_ref, cv_ref,
                    mixc_ref):
    A = ATTN_WIDTH
    x = x_ref[...]
    u_ref[...] = _pre_mod(x, mod_ref, gpre_ref, 1).astype(BF16)
    q_ref[...] = _dot(u_ref[...], win_ref[:, 0:A].astype(BF16)) * QK_SCALE_LOG2
    k = _dot(u_ref[...], win_ref[:, A:2 * A].astype(BF16))
    v = _dot(u_ref[...], win_ref[:, 2 * A:3 * A].astype(BF16))
    kb_ref[...] = k.astype(BF16)
    vb_ref[...] = v.astype(BF16)
    for h in range(N_HEADS):
        k_ref[pl.ds(h, CTX_TM, stride=N_HEADS), :] = k[:, h * V_DIM:(h + 1) * V_DIM]
        v_ref[pl.ds(h, CTX_TM, stride=N_HEADS), :] = v[:, h * V_DIM:(h + 1) * V_DIM]
    lam = _lambda(lam_ref)
    subg = subg_ref[...]
    n_units = (CTX_TM // SEQ) * N_HEADS
    n_conv_pieces = 3 * CONV_WIDTH // MXU_TILE

    def unit_slices(uid):
        b, h = divmod(uid, N_HEADS)
        return slice(b * SEQ, (b + 1) * SEQ), slice(h * V_DIM, (h + 1) * V_DIM)

    def conv_piece(i):
        c0 = 3 * A + i * MXU_TILE
        cv_ref[:, i * MXU_TILE:(i + 1) * MXU_TILE] = _dot(u_ref[...], win_ref[:, c0:c0 + MXU_TILE].astype(BF16))

    def mix_conv_piece(j):
        cols = slice(j * MXU_TILE, (j + 1) * MXU_TILE)
        mixc_ref[:, cols] = _dot(cat_ref[:, A:], wo_ref[A:, cols].astype(BF16))

    for uid in range(n_units):
        rows, cols = unit_slices(uid)
        q0, q1 = _split_maps(q_ref[rows, cols])
        _attn_scores(s_ref, uid, jnp.concatenate([q0, q1], axis=0), kb_ref[rows, cols])
    for uid in range(n_units):
        _attn_probs(s_ref, pd_ref, rz_ref, uid, lam)
        if uid < n_conv_pieces:
            conv_piece(uid)
    bg = cv_ref[:, 0:CONV_WIDTH]
    cg = cv_ref[:, CONV_WIDTH:2 * CONV_WIDTH]
    xc = cv_ref[:, 2 * CONV_WIDTH:]
    cat_ref[:, A:] = _short_conv(bg, cg, xc, convw_ref, SEQ).astype(BF16)
    for uid in range(n_units):
        rows, cols = unit_slices(uid)
        cat_ref[rows, cols] = _attn_values(pd_ref, rz_ref, uid, vb_ref[rows, cols], subg)
        if uid % 2 == 1:
            mix_conv_piece(uid // 2)
    mix = mixc_ref[...] + _dot(cat_ref[:, :A], wo_ref[:A, :].astype(BF16))
    gt = mod_ref[0, 5:6, :]
    o_ref[...] = x + gt * _rms(mix, gpost_ref[1:2, :])


def _mix_ctx(x2d, mod3, gpre, gpost, win, convw, lamqk, subg, wo):
    n = x2d.shape[0]
    tm = CTX_TM
    n_units = (tm // SEQ) * N_HEADS
    assert 3 * CONV_WIDTH // MXU_TILE <= n_units and D_MODEL // MXU_TILE == n_units // 2
    row = pl.BlockSpec((tm, D_MODEL), lambda t: (t, 0))
    kv = pl.BlockSpec((tm * N_HEADS, V_DIM), lambda t: (t, 0))
    return pl.pallas_call(
        _mix_ctx_kernel,
        grid=(n // tm,),
        in_specs=[
            row,
            pl.BlockSpec((1, 3 * N_SUB, D_MODEL), lambda t: (0, 0, 0)),
            _const_spec((N_SUB, D_MODEL)),
            _const_spec((N_SUB, D_MODEL)),
            _const_spec((D_MODEL, IN_WIDTH)),
            _const_spec((3, CONV_WIDTH)),
            _const_spec((4, HEAD_DIM)),
            _const_spec((1, V_DIM)),
            _const_spec((D_MODEL, D_MODEL)),
        ],
        out_specs=[row, kv, kv],
        out_shape=[
            jax.ShapeDtypeStruct((n, D_MODEL), F32),
            jax.ShapeDtypeStruct((n * N_HEADS, V_DIM), F32),
            jax.ShapeDtypeStruct((n * N_HEADS, V_DIM), F32),
        ],
        scratch_shapes=[
            pltpu.VMEM((tm, D_MODEL), BF16),
            pltpu.VMEM((tm, ATTN_WIDTH), F32),
            pltpu.VMEM((tm, ATTN_WIDTH), BF16),
            pltpu.VMEM((tm, ATTN_WIDTH), BF16),
            pltpu.VMEM((tm, D_MODEL), BF16),
            pltpu.VMEM((n_units, 2 * SEQ, SEQ), F32),
            pltpu.VMEM((n_units, SEQ, SEQ), BF16),
            pltpu.VMEM((n_units, SEQ, V_DIM), F32),
            pltpu.VMEM((tm, 3 * CONV_WIDTH), F32),
            pltpu.VMEM((tm, D_MODEL), F32),
        ],
        compiler_params=pltpu.CompilerParams(vmem_limit_bytes=VMEM_LIMIT),
        name="mixer_ctx",
    )(x2d, mod3, gpre, gpost, win, convw, lamqk, subg, wo)


def _rope(xh, cos, sin_lo, sin_hi):
    return xh * cos + pltpu.roll(xh, 16, 1) * sin_hi + pltpu.roll(xh, V_DIM - 16, 1) * sin_lo


def _rope_tables():
    t = jnp.arange(DEC_SEQ, dtype=jnp.int32)
    row = (t // GRID_W).astype(F32)
    col = (t % GRID_W).astype(F32)
    half = HEAD_DIM // 2
    freqs = ROPE_BASE ** (-jnp.arange(0, half, 2, dtype=F32) / half)
    ang = jnp.concatenate([row[:, None] * freqs, row[:, None] * freqs,
                           col[:, None] * freqs, col[:, None] * freqs], axis=-1)
    cos, sin = jnp.cos(ang), jnp.sin(ang)
    lane = jnp.arange(HEAD_DIM) % half
    first = (lane < half // 2)[None, :]
    sin_lo = jnp.where(first, -sin, 0.0)
    sin_hi = jnp.where(first, 0.0, sin)
    tile = lambda a: jnp.concatenate([a, a], axis=-1)
    return tile(cos), tile(sin_lo), tile(sin_hi)


def _lat_proj_kernel(x_ref, mod_ref, gpre_ref, win_ref, convw_ref, ck_ref, cv_ref, cos_ref, slo_ref, shi_ref,
                     qs_ref, kall_ref, vall_ref, conv_ref, u_ref):
    A, T, L = ATTN_WIDTH, DEC_SEQ, PAST_LEN
    u_ref[...] = _pre_mod(x_ref[...], mod_ref, gpre_ref, 1).astype(BF16)
    cos, slo, shi = cos_ref[...], slo_ref[...], shi_ref[...]
    q = _dot(u_ref[...], win_ref[:, 0:A].astype(BF16))
    for h in range(N_HEADS):
        cols = slice(h * V_DIM, (h + 1) * V_DIM)
        q0, q1 = _split_maps(_rope(q[:, cols], cos, slo, shi) * QK_SCALE_LOG2)
        for qb in range(T // LAT_QB):
            rows = slice(qb * LAT_QB, (qb + 1) * LAT_QB)
            qs_ref[0, h, qb, 0:LAT_QB, :] = q0[rows]
            qs_ref[0, h, qb, LAT_QB:, :] = q1[rows]
    k = _dot(u_ref[...], win_ref[:, A:2 * A].astype(BF16))
    for h in range(N_HEADS):
        cols = slice(h * V_DIM, (h + 1) * V_DIM)
        kall_ref[0, h, 0:L, :] = ck_ref[0, :, cols].astype(BF16)
        kall_ref[0, h, L:, :] = _rope(k[:, cols], cos, slo, shi).astype(BF16)
    v = _dot(u_ref[...], win_ref[:, 2 * A:3 * A].astype(BF16))
    for h in range(N_HEADS):
        cols = slice(h * V_DIM, (h + 1) * V_DIM)
        vall_ref[0, h, 0:L, :] = cv_ref[0, :, cols].astype(BF16)
        vall_ref[0, h, L:, :] = v[:, cols].astype(BF16)
    ub = u_ref[...]
    bg = _dot(ub, win_ref[:, 3 * A:3 * A + CONV_WIDTH].astype(BF16))
    cg = _dot(ub, win_ref[:, 3 * A + CONV_WIDTH:3 * A + 2 * CONV_WIDTH].astype(BF16))
    xc = _dot(ub, win_ref[:, 3 * A + 2 * CONV_WIDTH:].astype(BF16))
    conv_ref[...] = _short_conv(bg, cg, xc, convw_ref, T).astype(BF16)


def _lat_proj(x2d, mod3, gpre, win, convw, ck, cv):
    n = x2d.shape[0]
    T, L = DEC_SEQ, PAST_LEN
    nb = n // T
    nqb = T // LAT_QB
    cos, slo, shi = _rope_tables()
    cache = pl.BlockSpec((1, L, ATTN_WIDTH), lambda b: (b, 0, 0))
    return pl.pallas_call(
        _lat_proj_kernel,
        grid=(nb,),
        in_specs=[
            pl.BlockSpec((T, D_MODEL), lambda b: (b, 0)),
            pl.BlockSpec((1, 3 * N_SUB, D_MODEL), lambda b: (1 + b, 0, 0)),
            _const_spec((N_SUB, D_MODEL)),
            _const_spec((D_MODEL, IN_WIDTH)),
            _const_spec((3, CONV_WIDTH)),
            cache, cache,
            _const_spec((T, V_DIM)), _const_spec((T, V_DIM)), _const_spec((T, V_DIM)),
        ],
        out_specs=[
            pl.BlockSpec((1, N_HEADS, nqb, 2 * LAT_QB, V_DIM), lambda b: (b, 0, 0, 0, 0)),
            pl.BlockSpec((1, N_HEADS, L + T, V_DIM), lambda b: (b, 0, 0, 0)),
            pl.BlockSpec((1, N_HEADS, L + T, V_DIM), lambda b: (b, 0, 0, 0)),
            pl.BlockSpec((T, CONV_WIDTH), lambda b: (b, 0)),
        ],
        out_shape=[
            jax.ShapeDtypeStruct((nb, N_HEADS, nqb, 2 * LAT_QB, V_DIM), BF16),
            jax.ShapeDtypeStruct((nb, N_HEADS, L + T, V_DIM), BF16),
            jax.ShapeDtypeStruct((nb, N_HEADS, L + T, V_DIM), BF16),
            jax.ShapeDtypeStruct((n, CONV_WIDTH), BF16),
        ],
        scratch_shapes=[pltpu.VMEM((T, D_MODEL), BF16)],
        compiler_params=pltpu.CompilerParams(vmem_limit_bytes=VMEM_LIMIT),
        name="mixer_lat_proj",
    )(x2d, mod3, gpre, win, convw, ck, cv, cos, slo, shi)


def _lat_attn_kernel(x_ref, mod_ref, gpost_ref, qs_ref, kall_ref, vall_ref, conv_ref, lam_ref, subg_ref, wo_ref,
                     o_ref, cat_ref, s_ref, pd_ref, rz_ref):
    A = ATTN_WIDTH
    lam = _lambda(lam_ref)
    subg = subg_ref[...]

    def scores(h):
        _attn_scores(s_ref, h, qs_ref[0, h, 0], kall_ref[0, h])

    def probs(h):
        _attn_probs(s_ref, pd_ref, rz_ref, h, lam)

    def values(h):
        cat_ref[:, h * V_DIM:(h + 1) * V_DIM] = _attn_values(pd_ref, rz_ref, h, vall_ref[0, h], subg)

    scores(0)
    scores(1)
    mix_conv = _dot(conv_ref[...], wo_ref[A:, :].astype(BF16))
    for h in range(N_HEADS):
        probs(h)
        if h + 2 < N_HEADS:
            scores(h + 2)
        if h >= 1:
            values(h - 1)
    values(N_HEADS - 1)
    mix = mix_conv + _dot(cat_ref[...], wo_ref[:A, :].astype(BF16))
    gt = mod_ref[0, 5:6, :]
    o_ref[...] = x_ref[...] + gt * _rms(mix, gpost_ref[1:2, :])


def _lat_attn(x2d, mod3, gpost, qs, kall, vall, conv, lamqk, subg, wo):
    n = x2d.shape[0]
    T, L = DEC_SEQ, PAST_LEN
    nb = n // T
    nqb = T // LAT_QB
    row = pl.BlockSpec((LAT_QB, D_MODEL), lambda b, i: (b * nqb + i, 0))
    keys = pl.BlockSpec((1, N_HEADS, L + T, V_DIM), lambda b, i: (b, 0, 0, 0))
    return pl.pallas_call(
        _lat_attn_kernel,
        grid=(nb, nqb),
        in_specs=[
            row,
            pl.BlockSpec((1, 3 * N_SUB, D_MODEL), lambda b, i: (1 + b, 0, 0)),
            _const_spec((N_SUB, D_MODEL)),
            pl.BlockSpec((1, N_HEADS, 1, 2 * LAT_QB, V_DIM), lambda b, i: (b, 0, i, 0, 0)),
            keys, keys,
            pl.BlockSpec((LAT_QB, CONV_WIDTH), lambda b, i: (b * nqb + i, 0)),
            _const_spec((4, HEAD_DIM)),
            _const_spec((1, V_DIM)),
            _const_spec((D_MODEL, D_MODEL)),
        ],
        out_specs=row,
        out_shape=jax.ShapeDtypeStruct((n, D_MODEL), F32),
        scratch_shapes=[
            pltpu.VMEM((LAT_QB, ATTN_WIDTH), BF16),
            pltpu.VMEM((N_HEADS, 2 * LAT_QB, L + T), F32),
            pltpu.VMEM((N_HEADS, LAT_QB, L + T), BF16),
            pltpu.VMEM((N_HEADS, LAT_QB, V_DIM), F32),
        ],
        compiler_params=pltpu.CompilerParams(vmem_limit_bytes=VMEM_LIMIT),
        name="mixer_lat_attn",
    )(x2d, mod3, gpost, qs, kall, vall, conv, lamqk, subg, wo)


def kernel(x_prompt, x_sample, c, cache_k, cache_v, c_ctx, w_mod, b_mod, norm_pre, norm_post,
           ffn1_up, ffn1_down, ffn2_up, ffn2_down, w_in, conv_w, lam_qk, subln_g, w_o):
    batch, seq, _ = x_prompt.shape
    dec_batch, dec_seq, _ = x_sample.shape
    assert (seq, dec_seq) == (SEQ, DEC_SEQ) and cache_k.shape[1] == 1

    cond = jnp.zeros((COND_ROWS, D_MODEL), F32).at[0].set(c_ctx).at[1:1 + dec_batch].set(c)
    mod = _modulation(cond, w_mod[0], b_mod).reshape(COND_ROWS, 3 * N_SUB, D_MODEL)

    gpre, gpost = norm_pre[0], norm_post[0]
    up1, dn1, up2, dn2 = ffn1_up[0], ffn1_down[0], ffn2_up[0], ffn2_down[0]
    win, wo = w_in[0], w_o[0]
    convw, lamqk, subg = conv_w[0], lam_qk[0], subln_g

    ctx_row = lambda t: 0
    lat_row = lambda t: 1 + t // (DEC_SEQ // FFN_TM)

    xp = x_prompt.reshape(batch * seq, D_MODEL)
    xp = _ffn(xp, mod, gpre, gpost, up1, dn1, sub=0, row_of_tile=ctx_row)
    xp, new_k, new_v = _mix_ctx(xp, mod, gpre, gpost, win, convw, lamqk, subg, wo)
    xp = _ffn(xp, mod, gpre, gpost, up2, dn2, sub=2, row_of_tile=ctx_row)

    xs = x_sample.reshape(dec_batch * dec_seq, D_MODEL)
    ck = cache_k[:, 0].reshape(dec_batch, PAST_LEN, ATTN_WIDTH)
    cv = cache_v[:, 0].reshape(dec_batch, PAST_LEN, ATTN_WIDTH)
    xs = _ffn(xs, mod, gpre, gpost, up1, dn1, sub=0, row_of_tile=lat_row)
    qs, kall, vall, conv = _lat_proj(xs, mod, gpre, win, convw, ck, cv)
    xs = _lat_attn(xs, mod, gpost, qs, kall, vall, conv, lamqk, subg, wo)
    xs = _ffn(xs, mod, gpre, gpost, up2, dn2, sub=2, row_of_tile=lat_row)

    return (xp.reshape(batch, seq, D_MODEL), xs.reshape(dec_batch, dec_seq, D_MODEL),
            new_k.reshape(batch, 1, seq, N_HEADS, 2 * HEAD_DIM), new_v.reshape(batch, 1, seq, N_HEADS, V_DIM))
```

```python
import functools
import math

import jax
import jax.numpy as jnp
from jax import lax
from jax.experimental import pallas as pl
from jax.experimental.pallas import tpu as pltpu

D_MODEL = 1024
SEQ = 256
DEC_SEQ = 1024
PAST_LEN = 512
GRID_W = 64
ATTN_WIDTH = 512
CONV_WIDTH = 512
N_HEADS = 4
HEAD_DIM = 64
V_DIM = 128
D_FF = 2816
ROPE_BASE = 10000.0
EPS = 1e-6
N_SUB = 3
IN_WIDTH = 3 * ATTN_WIDTH + 3 * CONV_WIDTH
LAMBDA_INIT = 0.8 - 0.6 * math.exp(-0.3 * 0)
QK_SCALE_LOG2 = HEAD_DIM ** -0.5 * math.log2(math.e)

F32 = jnp.float32
BF16 = jnp.bfloat16

MXU_TILE = 256
COND_ROWS = 8
FFN_STEP_ROWS = 1024
FFN_SUB = 512
FFN_CHUNKS = ((0, 1024), (1024, 2048), (2048, D_FF))
W_PIECE = MXU_TILE
N_PIECES = D_FF // W_PIECE
CTX_TM = 512
LAT_QB = 256
VMEM_LIMIT = 60 * 1024 * 1024


def _rms(x, g):
    ms = jnp.mean(x * x, axis=-1, keepdims=True)
    return x * lax.rsqrt(ms + EPS) * g


def _dot(a, b):
    return jnp.dot(a, b, preferred_element_type=F32)


def _dot_nt(a, b):
    return lax.dot_general(a, b, (((1,), (1,)), ((), ())), preferred_element_type=F32)


def _const_spec(shape):
    nd = len(shape)
    return pl.BlockSpec(shape, lambda *_: (0,) * nd, pipeline_mode=pl.Buffered(1))


def _mod_kernel(c_ref, w_ref, b_ref, o_ref):
    c = c_ref[...]
    s = (c * jax.nn.sigmoid(c)).astype(BF16)
    o_ref[...] = _dot(s, w_ref[...].astype(BF16)) + b_ref[...]


def _modulation(cond, w_mod, b_mod):
    n_out = w_mod.shape[1]
    bn = 1024
    return pl.pallas_call(
        _mod_kernel,
        grid=(n_out // bn,),
        in_specs=[
            pl.BlockSpec((COND_ROWS, D_MODEL), lambda j: (0, 0)),
            pl.BlockSpec((D_MODEL, bn), lambda j: (0, j)),
            pl.BlockSpec((1, bn), lambda j: (0, j)),
        ],
        out_specs=pl.BlockSpec((COND_ROWS, bn), lambda j: (0, j)),
        out_shape=jax.ShapeDtypeStruct((COND_ROWS, n_out), F32),
        name="modulation",
    )(cond, w_mod, b_mod)


def _ffn_kernel(mod_ref, gpre_ref, gpost_ref, xp_hbm, xs_hbm, wup_hbm, wdn_hbm, op_hbm, os_hbm,
                wup_ref, wdn_ref, sa_ref, sb_ref, sd_ref, wsem, xbuf, xsem, obuf, osem, u_ref, acc_ref,
                *, sub, n_ctx, n_lat):
    t = pl.program_id(0)
    n = n_ctx + n_lat
    slot = t % 2
    R, S = FFN_STEP_ROWS, FFN_SUB
    n_sub = R // S

    def tile_dma(k, k_slot, action, into_vmem):
        def go(hbm_ref, blk):
            rows = hbm_ref.at[pl.ds(blk * R, R), :]
            if into_vmem:
                cp = pltpu.make_async_copy(rows, xbuf.at[k_slot], xsem.at[k_slot])
            else:
                cp = pltpu.make_async_copy(obuf.at[k_slot], rows, osem.at[k_slot])
            getattr(cp, action)()
        ctx_ref, lat_ref = (xp_hbm, xs_hbm) if into_vmem else (op_hbm, os_hbm)
        pl.when(k < n_ctx)(lambda: go(ctx_ref, k))
        pl.when(k >= n_ctx)(lambda: go(lat_ref, k - n_ctx))

    def weight_piece(j):
        ps = j % 2
        c = slice(j * W_PIECE, (j + 1) * W_PIECE)
        c2 = slice(D_FF + j * W_PIECE, D_FF + (j + 1) * W_PIECE)
        return (
            pltpu.make_async_copy(wup_hbm.at[:, c], sa_ref.at[ps], wsem.at[ps, 0]),
            pltpu.make_async_copy(wup_hbm.at[:, c2], sb_ref.at[ps], wsem.at[ps, 1]),
            pltpu.make_async_copy(wdn_hbm.at[c, :], sd_ref.at[ps], wsem.at[ps, 2]),
        )

    def start_piece(j):
        for cp in weight_piece(j):
            cp.start()

    def convert_piece(j):
        for cp in weight_piece(j):
            cp.wait()
        ps = j % 2
        wup_ref[:, j * W_PIECE:(j + 1) * W_PIECE] = sa_ref[ps].astype(BF16)
        wup_ref[:, D_FF + j * W_PIECE:D_FF + (j + 1) * W_PIECE] = sb_ref[ps].astype(BF16)
        wdn_ref[j * W_PIECE:(j + 1) * W_PIECE, :] = sd_ref[ps].astype(BF16)
        if j + 2 < N_PIECES:
            start_piece(j + 2)

    sh = mod_ref[0, 3 * sub:3 * sub + 1, :]
    sc = mod_ref[0, 3 * sub + 1:3 * sub + 2, :]
    gt = mod_ref[0, 3 * sub + 2:3 * sub + 3, :]

    def pre(s):
        x = xbuf[slot, s * S:(s + 1) * S, :]
        u_ref[s] = (_rms(x, gpre_ref[sub:sub + 1, :]) * (1.0 + sc) + sh).astype(BF16)

    def mm(s, ci):
        lo, hi = FFN_CHUNKS[ci]
        ub = u_ref[s]
        a = _dot(ub, wup_ref[:, lo:hi])
        b = _dot(ub, wup_ref[:, D_FF + lo:D_FF + hi])
        h = (a * jax.nn.sigmoid(a) * b).astype(BF16)
        part = _dot(h, wdn_ref[lo:hi, :])
        if ci == 0:
            acc_ref[s] = part
        else:
            acc_ref[s] += part

    def post(s):
        x = xbuf[slot, s * S:(s + 1) * S, :]
        obuf[slot, s * S:(s + 1) * S, :] = x + 0.5 * gt * _rms(acc_ref[s], gpost_ref[sub:sub + 1, :])

    pl.when(t == 0)(lambda: tile_dma(t, slot, "start", True))
    tile_dma(t, slot, "wait", True)
    pl.when(t + 1 < n)(lambda: tile_dma(t + 1, 1 - slot, "start", True))
    pl.when(t >= 2)(lambda: tile_dma(t - 2, slot, "wait", False))

    @pl.when(t == 0)
    def _():
        start_piece(0)
        start_piece(1)
        for s in range(n_sub):
            pre(s)
        need = [hi // W_PIECE for _, hi in FFN_CHUNKS]
        done = 0

        def convert_until(k):
            nonlocal done
            while done < k:
                convert_piece(done)
                done += 1

        convert_until(need[0])
        for ci in range(len(FFN_CHUNKS)):
            nxt = need[ci + 1] if ci + 1 < len(FFN_CHUNKS) else need[ci]
            for s in range(n_sub):
                mm(s, ci)
                convert_until(min(nxt, done + 2))
            convert_until(nxt)
        for s in range(n_sub):
            post(s)

    @pl.when(t > 0)
    def _():
        for s in range(n_sub):
            pre(s)
        for ci in range(len(FFN_CHUNKS)):
            for s in range(n_sub):
                mm(s, ci)
        for s in range(n_sub):
            post(s)

    tile_dma(t, slot, "start", False)

    @pl.when(t == n - 1)
    def _():
        if n >= 2:
            tile_dma(t - 1, 1 - slot, "wait", False)
        tile_dma(t, slot, "wait", False)


def _ffn(xp2d, xs2d, mod3, gpre, gpost, wup, wdn, *, sub):
    R = FFN_STEP_ROWS
    n_ctx, n_lat = xp2d.shape[0] // R, xs2d.shape[0] // R
    assert R == DEC_SEQ and xp2d.shape[0] % R == 0 and xs2d.shape[0] % R == 0
    mod_row = lambda t: jnp.where(t < n_ctx, 0, 1 + t - n_ctx)
    any_spec = pl.BlockSpec(memory_space=pl.ANY)
    return pl.pallas_call(
        functools.partial(_ffn_kernel, sub=sub, n_ctx=n_ctx, n_lat=n_lat),
        grid=(n_ctx + n_lat,),
        in_specs=[
            pl.BlockSpec((1, 3 * N_SUB, D_MODEL), lambda t: (mod_row(t), 0, 0)),
            _const_spec((N_SUB, D_MODEL)),
            _const_spec((N_SUB, D_MODEL)),
            any_spec, any_spec, any_spec, any_spec,
        ],
        out_specs=[any_spec, any_spec],
        out_shape=[jax.ShapeDtypeStruct(xp2d.shape, F32), jax.ShapeDtypeStruct(xs2d.shape, F32)],
        scratch_shapes=[
            pltpu.VMEM((D_MODEL, 2 * D_FF), BF16),
            pltpu.VMEM((D_FF, D_MODEL), BF16),
            pltpu.VMEM((2, D_MODEL, W_PIECE), F32),
            pltpu.VMEM((2, D_MODEL, W_PIECE), F32),
            pltpu.VMEM((2, W_PIECE, D_MODEL), F32),
            pltpu.SemaphoreType.DMA((2, 3)),
            pltpu.VMEM((2, R, D_MODEL), F32),
            pltpu.SemaphoreType.DMA((2,)),
            pltpu.VMEM((2, R, D_MODEL), F32),
            pltpu.SemaphoreType.DMA((2,)),
            pltpu.VMEM((R // FFN_SUB, FFN_SUB, D_MODEL), BF16),
            pltpu.VMEM((R // FFN_SUB, FFN_SUB, D_MODEL), F32),
        ],
        compiler_params=pltpu.CompilerParams(dimension_semantics=("arbitrary",), vmem_limit_bytes=VMEM_LIMIT),
        name=f"ffn{sub}",
    )(mod3, gpre, gpost, xp2d, xs2d, wup, wdn)


def _lambda(lam_ref):
    lq = lam_ref[...]
    l1 = jnp.sum(lq[0:1, :] * lq[1:2, :], axis=-1, keepdims=True)
    l2 = jnp.sum(lq[2:3, :] * lq[3:4, :], axis=-1, keepdims=True)
    return jnp.exp(l1) - jnp.exp(l2) + LAMBDA_INIT


def _split_maps(qh):
    lane = lax.broadcasted_iota(jnp.int32, qh.shape, 1)
    q0 = jnp.where(lane < HEAD_DIM, qh, 0.0).astype(BF16)
    q1 = jnp.where(lane >= HEAD_DIM, qh, 0.0).astype(BF16)
    return q0, q1


def _attn_scores(s_ref, idx, qs, kh):
    s_ref[idx] = _dot_nt(qs, kh)


def _attn_probs(s_ref, pd_ref, rz_ref, idx, lam):
    s = s_ref[idx]
    m = s.shape[0] // 2
    e = jnp.exp2(s - jnp.max(s, axis=-1, keepdims=True))
    z = jnp.sum(e, axis=-1, keepdims=True)
    c = lam * z[:m] / z[m:]
    pd_ref[idx] = (e[:m] - c * e[m:]).astype(BF16)
    rz_ref[idx] = jnp.broadcast_to(1.0 / z[:m], (m, V_DIM))


def _attn_values(pd_ref, rz_ref, idx, vh, subg):
    o = _dot(pd_ref[idx], vh) * rz_ref[idx]
    return (_rms(o, subg) * (1.0 - LAMBDA_INIT)).astype(BF16)


def _short_conv(bg, cg, xc, convw_ref, seq):
    g = cg * xc
    m = g.shape[0]
    row = lax.broadcasted_iota(jnp.int32, g.shape, 0) % seq
    g_prev = jnp.where(row == 0, 0.0, pltpu.roll(g, 1, 0))
    g_next = jnp.where(row == seq - 1, 0.0, pltpu.roll(g, m - 1, 0))
    conv = g_prev * convw_ref[0:1, :] + g * convw_ref[1:2, :] + g_next * convw_ref[2:3, :]
    return bg * conv


def _pre_mod(x, mod_ref, gpre_ref, sub):
    sh = mod_ref[0, 3 * sub:3 * sub + 1, :]
    sc = mod_ref[0, 3 * sub + 1:3 * sub + 2, :]
    return _rms(x, gpre_ref[sub:sub + 1, :]) * (1.0 + sc) + sh


def _mix_ctx_kernel(x_ref, mod_ref, gpre_ref, gpost_ref, win_ref, convw_ref, lam_ref, subg_ref, wo_ref,
                    o_ref, k_ref, v_ref, u_ref, q_ref, kb_ref, vb_ref, cat_ref, s_ref, pd_ref, rz_ref, cv_ref,
                    mixc_ref):
    A = ATTN_WIDTH
    x = x_ref[...]
    u_ref[...] = _pre_mod(x, mod_ref, gpre_ref, 1).astype(BF16)
    q_ref[...] = _dot(u_ref[...], win_ref[:, 0:A].astype(BF16)) * QK_SCALE_LOG2
    k = _dot(u_ref[...], win_ref[:, A:2 * A].astype(BF16))
    v = _dot(u_ref[...], win_ref[:, 2 * A:3 * A].astype(BF16))
    kb_ref[...] = k.astype(BF16)
    vb_ref[...] = v.astype(BF16)
    for h in range(N_HEADS):
        k_ref[pl.ds(h, CTX_TM, stride=N_HEADS), :] = k[:, h * V_DIM:(h + 1) * V_DIM]
        v_ref[pl.ds(h, CTX_TM, stride=N_HEADS), :] = v[:, h * V_DIM:(h + 1) * V_DIM]
    lam = _lambda(lam_ref)
    subg = subg_ref[...]
    n_units = (CTX_TM // SEQ) * N_HEADS
    n_conv_pieces = 3 * CONV_WIDTH // MXU_TILE

    def unit_slices(uid):
        b, h = divmod(uid, N_HEADS)
        return slice(b * SEQ, (b + 1) * SEQ), slice(h * V_DIM, (h + 1) * V_DIM)

    def conv_piece(i):
        c0 = 3 * A + i * MXU_TILE
        cv_ref[:, i * MXU_TILE:(i + 1) * MXU_TILE] = _dot(u_ref[...], win_ref[:, c0:c0 + MXU_TILE].astype(BF16))

    def mix_conv_piece(j):
        cols = slice(j * MXU_TILE, (j + 1) * MXU_TILE)
        mixc_ref[:, cols] = _dot(cat_ref[:, A:], wo_ref[A:, cols].astype(BF16))

    for uid in range(n_units):
        rows, cols = unit_slices(uid)
        q0, q1 = _split_maps(q_ref[rows, cols])
        _attn_scores(s_ref, uid, jnp.concatenate([q0, q1], axis=0), kb_ref[rows, cols])
    for uid in range(n_units):
        _attn_probs(s_ref, pd_ref, rz_ref, uid, lam)
        if uid < n_conv_pieces:
            conv_piece(uid)
    bg = cv_ref[:, 0:CONV_WIDTH]
    cg = cv_ref[:, CONV_WIDTH:2 * CONV_WIDTH]
    xc = cv_ref[:, 2 * CONV_WIDTH:]
    cat_ref[:, A:] = _short_conv(bg, cg, xc, convw_ref, SEQ).astype(BF16)
    for uid in range(n_units):
        rows, cols = unit_slices(uid)
        cat_ref[rows, cols] = _attn_values(pd_ref, rz_ref, uid, vb_ref[rows, cols], subg)
        if uid % 2 == 1:
            mix_conv_piece(uid // 2)
    mix = mixc_ref[...] + _dot(cat_ref[:, :A], wo_ref[:A, :].astype(BF16))
    gt = mod_ref[0, 5:6, :]
    o_ref[...] = x + gt * _rms(mix, gpost_ref[1:2, :])


def _mix_ctx(x2d, mod3, gpre, gpost, win, convw, lamqk, subg, wo):
    n = x2d.shape[0]
    tm = CTX_TM
    n_units = (tm // SEQ) * N_HEADS
    assert 3 * CONV_WIDTH // MXU_TILE <= n_units and D_MODEL // MXU_TILE == n_units // 2
    row = pl.BlockSpec((tm, D_MODEL), lambda t: (t, 0))
    kv = pl.BlockSpec((tm * N_HEADS, V_DIM), lambda t: (t, 0))
    return pl.pallas_call(
        _mix_ctx_kernel,
        grid=(n // tm,),
        in_specs=[
            row,
            pl.BlockSpec((1, 3 * N_SUB, D_MODEL), lambda t: (0, 0, 0)),
            _const_spec((N_SUB, D_MODEL)),
            _const_spec((N_SUB, D_MODEL)),
            _const_spec((D_MODEL, IN_WIDTH)),
            _const_spec((3, CONV_WIDTH)),
            _const_spec((4, HEAD_DIM)),
            _const_spec((1, V_DIM)),
            _const_spec((D_MODEL, D_MODEL)),
        ],
        out_specs=[row, kv, kv],
        out_shape=[
            jax.ShapeDtypeStruct((n, D_MODEL), F32),
            jax.ShapeDtypeStruct((n * N_HEADS, V_DIM), F32),
            jax.ShapeDtypeStruct((n * N_HEADS, V_DIM), F32),
        ],
        scratch_shapes=[
            pltpu.VMEM((tm, D_MODEL), BF16),
            pltpu.VMEM((tm, ATTN_WIDTH), F32),
            pltpu.VMEM((tm, ATTN_WIDTH), BF16),
            pltpu.VMEM((tm, ATTN_WIDTH), BF16),
            pltpu.VMEM((tm, D_MODEL), BF16),
            pltpu.VMEM((n_units, 2 * SEQ, SEQ), F32),
            pltpu.VMEM((n_units, SEQ, SEQ), BF16),
            pltpu.VMEM((n_units, SEQ, V_DIM), F32),
            pltpu.VMEM((tm, 3 * CONV_WIDTH), F32),
            pltpu.VMEM((tm, D_MODEL), F32),
        ],
        compiler_params=pltpu.CompilerParams(vmem_limit_bytes=VMEM_LIMIT),
        name="mixer_ctx",
    )(x2d, mod3, gpre, gpost, win, convw, lamqk, subg, wo)


def _rope(xh, cos, sin_lo, sin_hi):
    return xh * cos + pltpu.roll(xh, 16, 1) * sin_hi + pltpu.roll(xh, V_DIM - 16, 1) * sin_lo


def _rope_tables():
    t = jnp.arange(DEC_SEQ, dtype=jnp.int32)
    row = (t // GRID_W).astype(F32)
    col = (t % GRID_W).astype(F32)
    half = HEAD_DIM // 2
    freqs = ROPE_BASE ** (-jnp.arange(0, half, 2, dtype=F32) / half)
    ang = jnp.concatenate([row[:, None] * freqs, row[:, None] * freqs,
                           col[:, None] * freqs, col[:, None] * freqs], axis=-1)
    cos, sin = jnp.cos(ang), jnp.sin(ang)
    lane = jnp.arange(HEAD_DIM) % half
    first = (lane < half // 2)[None, :]
    sin_lo = jnp.where(first, -sin, 0.0)
    sin_hi = jnp.where(first, 0.0, sin)
    tile = lambda a: jnp.concatenate([a, a], axis=-1)
    return tile(cos), tile(sin_lo), tile(sin_hi)


def _lat_proj_kernel(x_ref, mod_ref, gpre_ref, win_ref, convw_ref, ck_ref, cv_ref, cos_ref, slo_ref, shi_ref,
                     qs_ref, kall_ref, vall_ref, conv_ref, u_ref):
    A, T, L = ATTN_WIDTH, DEC_SEQ, PAST_LEN
    u_ref[...] = _pre_mod(x_ref[...], mod_ref, gpre_ref, 1).astype(BF16)
    cos, slo, shi = cos_ref[...], slo_ref[...], shi_ref[...]
    q = _dot(u_ref[...], win_ref[:, 0:A].astype(BF16))
    for h in range(N_HEADS):
        cols = slice(h * V_DIM, (h + 1) * V_DIM)
        q0, q1 = _split_maps(_rope(q[:, cols], cos, slo, shi) * QK_SCALE_LOG2)
        for qb in range(T // LAT_QB):
            rows = slice(qb * LAT_QB, (qb + 1) * LAT_QB)
            qs_ref[0, h, qb, 0:LAT_QB, :] = q0[rows]
            qs_ref[0, h, qb, LAT_QB:, :] = q1[rows]
    k = _dot(u_ref[...], win_ref[:, A:2 * A].astype(BF16))
    for h in range(N_HEADS):
        cols = slice(h * V_DIM, (h + 1) * V_DIM)
        kall_ref[0, h, 0:L, :] = ck_ref[0, :, cols].astype(BF16)
        kall_ref[0, h, L:, :] = _rope(k[:, cols], cos, slo, shi).astype(BF16)
    v = _dot(u_ref[...], win_ref[:, 2 * A:3 * A].astype(BF16))
    for h in range(N_HEADS):
        cols = slice(h * V_DIM, (h + 1) * V_DIM)
        vall_ref[0, h, 0:L, :] = cv_ref[0, :, cols].astype(BF16)
        vall_ref[0, h, L:, :] = v[:, cols].astype(BF16)
    ub = u_ref[...]
    bg = _dot(ub, win_ref[:, 3 * A:3 * A + CONV_WIDTH].astype(BF16))
    cg = _dot(ub, win_ref[:, 3 * A + CONV_WIDTH:3 * A + 2 * CONV_WIDTH].astype(BF16))
    xc = _dot(ub, win_ref[:, 3 * A + 2 * CONV_WIDTH:].astype(BF16))
    conv_ref[...] = _short_conv(bg, cg, xc, convw_ref, T).astype(BF16)


def _lat_proj(x2d, mod3, gpre, win, convw, ck, cv):
    n = x2d.shape[0]
    T, L = DEC_SEQ, PAST_LEN
    nb = n // T
    nqb = T // LAT_QB
    cos, slo, shi = _rope_tables()
    cache = pl.BlockSpec((1, L, ATTN_WIDTH), lambda b: (b, 0, 0))
    return pl.pallas_call(
        _lat_proj_kernel,
        grid=(nb,),
        in_specs=[
            pl.BlockSpec((T, D_MODEL), lambda b: (b, 0)),
            pl.BlockSpec((1, 3 * N_SUB, D_MODEL), lambda b: (1 + b, 0, 0)),
            _const_spec((N_SUB, D_MODEL)),
            _const_spec((D_MODEL, IN_WIDTH)),
            _const_spec((3, CONV_WIDTH)),
            cache, cache,
            _const_spec((T, V_DIM)), _const_spec((T, V_DIM)), _const_spec((T, V_DIM)),
        ],
        out_specs=[
            pl.BlockSpec((1, N_HEADS, nqb, 2 * LAT_QB, V_DIM), lambda b: (b, 0, 0, 0, 0)),
            pl.BlockSpec((1, N_HEADS, L + T, V_DIM), lambda b: (b, 0, 0, 0)),
            pl.BlockSpec((1, N_HEADS, L + T, V_DIM), lambda b: (b, 0, 0, 0)),
            pl.BlockSpec((T, CONV_WIDTH), lambda b: (b, 0)),
        ],
        out_shape=[
            jax.ShapeDtypeStruct((nb, N_HEADS, nqb, 2 * LAT_QB, V_DIM), BF16),
            jax.ShapeDtypeStruct((nb, N_HEADS, L + T, V_DIM), BF16),
            jax.ShapeDtypeStruct((nb, N_HEADS, L + T, V_DIM), BF16),
            jax.ShapeDtypeStruct((n, CONV_WIDTH), BF16),
        ],
        scratch_shapes=[pltpu.VMEM((T, D_MODEL), BF16)],
        compiler_params=pltpu.CompilerParams(vmem_limit_bytes=VMEM_LIMIT),
        name="mixer_lat_proj",
    )(x2d, mod3, gpre, win, convw, ck, cv, cos, slo, shi)


def _lat_attn_kernel(x_ref, mod_ref, gpost_ref, qs_ref, kall_ref, vall_ref, conv_ref, lam_ref, subg_ref, wo_ref,
                     o_ref, cat_ref, s_ref, pd_ref, rz_ref):
    A = ATTN_WIDTH
    lam = _lambda(lam_ref)
    subg = subg_ref[...]

    def scores(h):
        _attn_scores(s_ref, h, qs_ref[0, h, 0], kall_ref[0, h])

    def probs(h):
        _attn_probs(s_ref, pd_ref, rz_ref, h, lam)

    def values(h):
        cat_ref[:, h * V_DIM:(h + 1) * V_DIM] = _attn_values(pd_ref, rz_ref, h, vall_ref[0, h], subg)

    scores(0)
    scores(1)
    mix_conv = _dot(conv_ref[...], wo_ref[A:, :].astype(BF16))
    for h in range(N_HEADS):
        probs(h)
        if h + 2 < N_HEADS:
            scores(h + 2)
        if h >= 1:
            values(h - 1)
    values(N_HEADS - 1)
    mix = mix_conv + _dot(cat_ref[...], wo_ref[:A, :].astype(BF16))
    gt = mod_ref[0, 5:6, :]
    o_ref[...] = x_ref[...] + gt * _rms(mix, gpost_ref[1:2, :])


def _lat_attn(x2d, mod3, gpost, qs, kall, vall, conv, lamqk, subg, wo):
    n = x2d.shape[0]
    T, L = DEC_SEQ, PAST_LEN
    nb = n // T
    nqb = T // LAT_QB
    row = pl.BlockSpec((LAT_QB, D_MODEL), lambda b, i: (b * nqb + i, 0))
    keys = pl.BlockSpec((1, N_HEADS, L + T, V_DIM), lambda b, i: (b, 0, 0, 0))
    return pl.pallas_call(
        _lat_attn_kernel,
        grid=(nb, nqb),
        in_specs=[
            row,
            pl.BlockSpec((1, 3 * N_SUB, D_MODEL), lambda b, i: (1 + b, 0, 0)),
            _const_spec((N_SUB, D_MODEL)),
            pl.BlockSpec((1, N_HEADS, 1, 2 * LAT_QB, V_DIM), lambda b, i: (b, 0, i, 0, 0)),
            keys, keys,
            pl.BlockSpec((LAT_QB, CONV_WIDTH), lambda b, i: (b * nqb + i, 0)),
            _const_spec((4, HEAD_DIM)),
            _const_spec((1, V_DIM)),
            _const_spec((D_MODEL, D_MODEL)),
        ],
        out_specs=row,
        out_shape=jax.ShapeDtypeStruct((n, D_MODEL), F32),
        scratch_shapes=[
            pltpu.VMEM((LAT_QB, ATTN_WIDTH), BF16),
            pltpu.VMEM((N_HEADS, 2 * LAT_QB, L + T), F32),
            pltpu.VMEM((N_HEADS, LAT_QB, L + T), BF16),
            pltpu.VMEM((N_HEADS, LAT_QB, V_DIM), F32),
        ],
        compiler_params=pltpu.CompilerParams(vmem_limit_bytes=VMEM_LIMIT),
        name="mixer_lat_attn",
    )(x2d, mod3, gpost, qs, kall, vall, conv, lamqk, subg, wo)


def kernel(x_prompt, x_sample, c, cache_k, cache_v, c_ctx, w_mod, b_mod, norm_pre, norm_post,
           ffn1_up, ffn1_down, ffn2_up, ffn2_down, w_in, conv_w, lam_qk, subln_g, w_o):
    batch, seq, _ = x_prompt.shape
    dec_batch, dec_seq, _ = x_sample.shape
    assert (seq, dec_seq) == (SEQ, DEC_SEQ) and cache_k.shape[1] == 1

    cond = jnp.zeros((COND_ROWS, D_MODEL), F32).at[0].set(c_ctx).at[1:1 + dec_batch].set(c)
    mod = _modulation(cond, w_mod[0], b_mod).reshape(COND_ROWS, 3 * N_SUB, D_MODEL)

    gpre, gpost = norm_pre[0], norm_post[0]
    up1, dn1, up2, dn2 = ffn1_up[0], ffn1_down[0], ffn2_up[0], ffn2_down[0]
    win, wo = w_in[0], w_o[0]
    convw, lamqk, subg = conv_w[0], lam_qk[0], subln_g

    xp = x_prompt.reshape(batch * seq, D_MODEL)
    xs = x_sample.reshape(dec_batch * dec_seq, D_MODEL)
    ck = cache_k[:, 0].reshape(dec_batch, PAST_LEN, ATTN_WIDTH)
    cv = cache_v[:, 0].reshape(dec_batch, PAST_LEN, ATTN_WIDTH)

    xp, xs = _ffn(xp, xs, mod, gpre, gpost, up1, dn1, sub=0)
    xp, new_k, new_v = _mix_ctx(xp, mod, gpre, gpost, win, convw, lamqk, subg, wo)
    qs, kall, vall, conv = _lat_proj(xs, mod, gpre, win, convw, ck, cv)
    xs = _lat_attn(xs, mod, gpost, qs, kall, vall, conv, lamqk, subg, wo)
    xp, xs = _ffn(xp, xs, mod, gpre, gpost, up2, dn2, sub=2)

    return (xp.reshape(batch, seq, D_MODEL), xs.reshape(dec_batch, dec_seq, D_MODEL),
            new_k.reshape(batch, 1, seq, N_HEADS, 2 * HEAD_DIM), new_v.reshape(batch, 1, seq, N_HEADS, V_DIM))
```

```python
import functools
import math

import jax
import jax.numpy as jnp
from jax import lax
from jax.experimental import pallas as pl
from jax.experimental.pallas import tpu as pltpu

D_MODEL = 1024
SEQ = 256
DEC_SEQ = 1024
PAST_LEN = 512
GRID_W = 64
ATTN_WIDTH = 512
CONV_WIDTH = 512
N_HEADS = 4
HEAD_DIM = 64
V_DIM = 128
D_FF = 2816
ROPE_BASE = 10000.0
EPS = 1e-6
N_SUB = 3
IN_WIDTH = 3 * ATTN_WIDTH + 3 * CONV_WIDTH
LAMBDA_INIT = 0.8 - 0.6 * math.exp(-0.3 * 0)
QK_SCALE_LOG2 = HEAD_DIM ** -0.5 * math.log2(math.e)

F32 = jnp.float32
BF16 = jnp.bfloat16

MXU_TILE = 256
COND_ROWS = 8
FFN_STEP_ROWS = 1024
FFN_SUB = 512
FFN_CHUNKS = ((0, 1024), (1024, 2048), (2048, D_FF))
W_PIECE = MXU_TILE
N_PIECES = D_FF // W_PIECE
CTX_TM = 512
LAT_QB = 256
VMEM_LIMIT = 60 * 1024 * 1024


def _rms(x, g):
    ms = jnp.mean(x * x, axis=-1, keepdims=True)
    return x * lax.rsqrt(ms + EPS) * g


def _dot(a, b):
    return jnp.dot(a, b, preferred_element_type=F32)


def _dot_nt(a, b):
    return lax.dot_general(a, b, (((1,), (1,)), ((), ())), preferred_element_type=F32)


def _const_spec(shape):
    nd = len(shape)
    return pl.BlockSpec(shape, lambda *_: (0,) * nd, pipeline_mode=pl.Buffered(1))


def _mod_kernel(c_ref, w_ref, b_ref, o_ref):
    c = c_ref[...]
    s = (c * jax.nn.sigmoid(c)).astype(BF16)
    o_ref[...] = _dot(s, w_ref[...].astype(BF16)) + b_ref[...]


def _modulation(cond, w_mod, b_mod):
    n_out = w_mod.shape[1]
    bn = 1024
    return pl.pallas_call(
        _mod_kernel,
        grid=(n_out // bn,),
        in_specs=[
            pl.BlockSpec((COND_ROWS, D_MODEL), lambda j: (0, 0)),
            pl.BlockSpec((D_MODEL, bn), lambda j: (0, j)),
            pl.BlockSpec((1, bn), lambda j: (0, j)),
        ],
        out_specs=pl.BlockSpec((COND_ROWS, bn), lambda j: (0, j)),
        out_shape=jax.ShapeDtypeStruct((COND_ROWS, n_out), F32),
        name="modulation",
    )(cond, w_mod, b_mod)


def _ffn_kernel(mod_ref, gpre_ref, gpost_ref, xp_hbm, xs_hbm, wup_hbm, wdn_hbm, op_hbm, os_hbm,
                wup_ref, wdn_ref, sa_ref, sb_ref, sd_ref, wsem, xbuf, xsem, obuf, osem, u_ref, acc_ref,
                *, sub, n_ctx, n_lat):
    t = pl.program_id(0)
    n = n_ctx + n_lat
    slot = t % 2
    R, S = FFN_STEP_ROWS, FFN_SUB
    n_sub = R // S

    def tile_dma(k, k_slot, action, into_vmem):
        def go(hbm_ref, blk):
            rows = hbm_ref.at[pl.ds(blk * R, R), :]
            if into_vmem:
                cp = pltpu.make_async_copy(rows, xbuf.at[k_slot], xsem.at[k_slot])
            else:
                cp = pltpu.make_async_copy(obuf.at[k_slot], rows, osem.at[k_slot])
            getattr(cp, action)()
        ctx_ref, lat_ref = (xp_hbm, xs_hbm) if into_vmem else (op_hbm, os_hbm)
        pl.when(k < n_ctx)(lambda: go(ctx_ref, k))
        pl.when(k >= n_ctx)(lambda: go(lat_ref, k - n_ctx))

    def weight_piece(j):
        ps = j % 2
        c = pl.ds(pl.multiple_of(j * W_PIECE, W_PIECE), W_PIECE)
        c2 = pl.ds(pl.multiple_of(D_FF + j * W_PIECE, W_PIECE), W_PIECE)
        return (
            pltpu.make_async_copy(wup_hbm.at[:, c], sa_ref.at[ps], wsem.at[ps, 0]),
            pltpu.make_async_copy(wup_hbm.at[:, c2], sb_ref.at[ps], wsem.at[ps, 1]),
            pltpu.make_async_copy(wdn_hbm.at[c, :], sd_ref.at[ps], wsem.at[ps, 2]),
        )

    def start_piece(j):
        for cp in weight_piece(j):
            cp.start()

    def convert_piece(j, carry):
        for cp in weight_piece(j):
            cp.wait()
        ps = j % 2
        wup_ref[0, j] = sa_ref[ps].astype(BF16)
        wup_ref[1, j] = sb_ref[ps].astype(BF16)
        wdn_ref[j] = sd_ref[ps].astype(BF16)
        pl.when(j + 2 < N_PIECES)(lambda: start_piece(j + 2))
        return carry

    sh = mod_ref[0, 3 * sub:3 * sub + 1, :]
    sc = mod_ref[0, 3 * sub + 1:3 * sub + 2, :]
    gt = mod_ref[0, 3 * sub + 2:3 * sub + 3, :]

    def pre(s):
        x = xbuf[slot, s * S:(s + 1) * S, :]
        u_ref[s] = (_rms(x, gpre_ref[sub:sub + 1, :]) * (1.0 + sc) + sh).astype(BF16)

    def mm(s, ci):
        lo, hi = FFN_CHUNKS[ci]
        pieces = range(lo // W_PIECE, hi // W_PIECE)
        ub = u_ref[s]
        a = jnp.concatenate([_dot(ub, wup_ref[0, j]) for j in pieces], axis=1)
        b = jnp.concatenate([_dot(ub, wup_ref[1, j]) for j in pieces], axis=1)
        h = (a * jax.nn.sigmoid(a) * b).astype(BF16)
        part = _dot(h, wdn_ref[pieces.start:pieces.stop].reshape(hi - lo, D_MODEL))
        if ci == 0:
            acc_ref[s] = part
        else:
            acc_ref[s] += part

    def post(s):
        x = xbuf[slot, s * S:(s + 1) * S, :]
        obuf[slot, s * S:(s + 1) * S, :] = x + 0.5 * gt * _rms(acc_ref[s], gpost_ref[sub:sub + 1, :])

    pl.when(t == 0)(lambda: tile_dma(t, slot, "start", True))
    tile_dma(t, slot, "wait", True)
    pl.when(t + 1 < n)(lambda: tile_dma(t + 1, 1 - slot, "start", True))
    pl.when(t >= 2)(lambda: tile_dma(t - 2, slot, "wait", False))

    @pl.when(t == 0)
    def _():
        start_piece(0)
        start_piece(1)
        lax.fori_loop(0, N_PIECES, convert_piece, 0)

    for s in range(n_sub):
        pre(s)
    for ci in range(len(FFN_CHUNKS)):
        for s in range(n_sub):
            mm(s, ci)
    for s in range(n_sub):
        post(s)

    tile_dma(t, slot, "start", False)

    @pl.when(t == n - 1)
    def _():
        if n >= 2:
            tile_dma(t - 1, 1 - slot, "wait", False)
        tile_dma(t, slot, "wait", False)


def _ffn(xp2d, xs2d, mod3, gpre, gpost, wup, wdn, *, sub):
    R = FFN_STEP_ROWS
    n_ctx, n_lat = xp2d.shape[0] // R, xs2d.shape[0] // R
    assert R == DEC_SEQ and xp2d.shape[0] % R == 0 and xs2d.shape[0] % R == 0
    mod_row = lambda t: jnp.where(t < n_ctx, 0, 1 + t - n_ctx)
    any_spec = pl.BlockSpec(memory_space=pl.ANY)
    return pl.pallas_call(
        functools.partial(_ffn_kernel, sub=sub, n_ctx=n_ctx, n_lat=n_lat),
        grid=(n_ctx + n_lat,),
        in_specs=[
            pl.BlockSpec((1, 3 * N_SUB, D_MODEL), lambda t: (mod_row(t), 0, 0)),
            _const_spec((N_SUB, D_MODEL)),
            _const_spec((N_SUB, D_MODEL)),
            any_spec, any_spec, any_spec, any_spec,
        ],
        out_specs=[any_spec, any_spec],
        out_shape=[jax.ShapeDtypeStruct(xp2d.shape, F32), jax.ShapeDtypeStruct(xs2d.shape, F32)],
        scratch_shapes=[
            pltpu.VMEM((2, N_PIECES, D_MODEL, W_PIECE), BF16),
            pltpu.VMEM((N_PIECES, W_PIECE, D_MODEL), BF16),
            pltpu.VMEM((2, D_MODEL, W_PIECE), F32),
            pltpu.VMEM((2, D_MODEL, W_PIECE), F32),
            pltpu.VMEM((2, W_PIECE, D_MODEL), F32),
            pltpu.SemaphoreType.DMA((2, 3)),
            pltpu.VMEM((2, R, D_MODEL), F32),
            pltpu.SemaphoreType.DMA((2,)),
            pltpu.VMEM((2, R, D_MODEL), F32),
            pltpu.SemaphoreType.DMA((2,)),
            pltpu.VMEM((R // FFN_SUB, FFN_SUB, D_MODEL), BF16),
            pltpu.VMEM((R // FFN_SUB, FFN_SUB, D_MODEL), F32),
        ],
        compiler_params=pltpu.CompilerParams(dimension_semantics=("arbitrary",), vmem_limit_bytes=VMEM_LIMIT),
        name=f"ffn{sub}",
    )(mod3, gpre, gpost, xp2d, xs2d, wup, wdn)


def _lambda(lam_ref):
    lq = lam_ref[...]
    l1 = jnp.sum(lq[0:1, :] * lq[1:2, :], axis=-1, keepdims=True)
    l2 = jnp.sum(lq[2:3, :] * lq[3:4, :], axis=-1, keepdims=True)
    return jnp.exp(l1) - jnp.exp(l2) + LAMBDA_INIT


def _split_maps(qh):
    lane = lax.broadcasted_iota(jnp.int32, qh.shape, 1)
    q0 = jnp.where(lane < HEAD_DIM, qh, 0.0).astype(BF16)
    q1 = jnp.where(lane >= HEAD_DIM, qh, 0.0).astype(BF16)
    return q0, q1


def _attn_scores(s_ref, idx, qs, kh):
    s_ref[idx] = _dot_nt(qs, kh)


def _attn_probs(s_ref, pd_ref, rz_ref, idx, lam):
    s = s_ref[idx]
    m = s.shape[0] // 2
    e = jnp.exp2(s - jnp.max(s, axis=-1, keepdims=True))
    z = jnp.sum(e, axis=-1, keepdims=True)
    c = lam * z[:m] / z[m:]
    pd_ref[idx] = (e[:m] - c * e[m:]).astype(BF16)
    rz_ref[idx] = jnp.broadcast_to(1.0 / z[:m], (m, V_DIM))


def _attn_values(pd_ref, rz_ref, idx, vh, subg):
    o = _dot(pd_ref[idx], vh) * rz_ref[idx]
    return (_rms(o, subg) * (1.0 - LAMBDA_INIT)).astype(BF16)


def _short_conv(bg, cg, xc, convw_ref, seq):
    g = cg * xc
    m = g.shape[0]
    row = lax.broadcasted_iota(jnp.int32, g.shape, 0) % seq
    g_prev = jnp.where(row == 0, 0.0, pltpu.roll(g, 1, 0))
    g_next = jnp.where(row == seq - 1, 0.0, pltpu.roll(g, m - 1, 0))
    conv = g_prev * convw_ref[0:1, :] + g * convw_ref[1:2, :] + g_next * convw_ref[2:3, :]
    return bg * conv


def _pre_mod(x, mod_ref, gpre_ref, sub):
    sh = mod_ref[0, 3 * sub:3 * sub + 1, :]
    sc = mod_ref[0, 3 * sub + 1:3 * sub + 2, :]
    return _rms(x, gpre_ref[sub:sub + 1, :]) * (1.0 + sc) + sh


def _mix_ctx_kernel(x_ref, mod_ref, gpre_ref, gpost_ref, win_ref, convw_ref, lam_ref, subg_ref, wo_ref,
                    o_ref, k_ref, v_ref, u_ref, q_ref, kb_ref, vb_ref, cat_ref, s_ref, pd_ref, rz_ref, cv_ref,
                    mixc_ref):
    A = ATTN_WIDTH
    x = x_ref[...]
    u_ref[...] = _pre_mod(x, mod_ref, gpre_ref, 1).astype(BF16)
    q_ref[...] = _dot(u_ref[...], win_ref[:, 0:A].astype(BF16)) * QK_SCALE_LOG2
    k = _dot(u_ref[...], win_ref[:, A:2 * A].astype(BF16))
    v = _dot(u_ref[...], win_ref[:, 2 * A:3 * A].astype(BF16))
    kb_ref[...] = k.astype(BF16)
    vb_ref[...] = v.astype(BF16)
    for h in range(N_HEADS):
        k_ref[pl.ds(h, CTX_TM, stride=N_HEADS), :] = k[:, h * V_DIM:(h + 1) * V_DIM]
        v_ref[pl.ds(h, CTX_TM, stride=N_HEADS), :] = v[:, h * V_DIM:(h + 1) * V_DIM]
    lam = _lambda(lam_ref)
    subg = subg_ref[...]
    n_units = (CTX_TM // SEQ) * N_HEADS
    n_conv_pieces = 3 * CONV_WIDTH // MXU_TILE

    def unit_slices(uid):
        b, h = divmod(uid, N_HEADS)
        return slice(b * SEQ, (b + 1) * SEQ), slice(h * V_DIM, (h + 1) * V_DIM)

    def conv_piece(i):
        c0 = 3 * A + i * MXU_TILE
        cv_ref[:, i * MXU_TILE:(i + 1) * MXU_TILE] = _dot(u_ref[...], win_ref[:, c0:c0 + MXU_TILE].astype(BF16))

    def mix_conv_piece(j):
        cols = slice(j * MXU_TILE, (j + 1) * MXU_TILE)
        mixc_ref[:, cols] = _dot(cat_ref[:, A:], wo_ref[A:, cols].astype(BF16))

    for uid in range(n_units):
        rows, cols = unit_slices(uid)
        q0, q1 = _split_maps(q_ref[rows, cols])
        _attn_scores(s_ref, uid, jnp.concatenate([q0, q1], axis=0), kb_ref[rows, cols])
    for uid in range(n_units):
        _attn_probs(s_ref, pd_ref, rz_ref, uid, lam)
        if uid < n_conv_pieces:
            conv_piece(uid)
    bg = cv_ref[:, 0:CONV_WIDTH]
    cg = cv_ref[:, CONV_WIDTH:2 * CONV_WIDTH]
    xc = cv_ref[:, 2 * CONV_WIDTH:]
    cat_ref[:, A:] = _short_conv(bg, cg, xc, convw_ref, SEQ).astype(BF16)
    for uid in range(n_units):
        rows, cols = unit_slices(uid)
        cat_ref[rows, cols] = _attn_values(pd_ref, rz_ref, uid, vb_ref[rows, cols], subg)
        if uid % 2 == 1:
            mix_conv_piece(uid // 2)
    mix = mixc_ref[...] + _dot(cat_ref[:, :A], wo_ref[:A, :].astype(BF16))
    gt = mod_ref[0, 5:6, :]
    o_ref[...] = x + gt * _rms(mix, gpost_ref[1:2, :])


def _mix_ctx(x2d, mod3, gpre, gpost, win, convw, lamqk, subg, wo):
    n = x2d.shape[0]
    tm = CTX_TM
    n_units = (tm // SEQ) * N_HEADS
    assert 3 * CONV_WIDTH // MXU_TILE <= n_units and D_MODEL // MXU_TILE == n_units // 2
    row = pl.BlockSpec((tm, D_MODEL), lambda t: (t, 0))
    kv = pl.BlockSpec((tm * N_HEADS, V_DIM), lambda t: (t, 0))
    return pl.pallas_call(
        _mix_ctx_kernel,
        grid=(n // tm,),
        in_specs=[
            row,
            pl.BlockSpec((1, 3 * N_SUB, D_MODEL), lambda t: (0, 0, 0)),
            _const_spec((N_SUB, D_MODEL)),
            _const_spec((N_SUB, D_MODEL)),
            _const_spec((D_MODEL, IN_WIDTH)),
            _const_spec((3, CONV_WIDTH)),
            _const_spec((4, HEAD_DIM)),
            _const_spec((1, V_DIM)),
            _const_spec((D_MODEL, D_MODEL)),
        ],
        out_specs=[row, kv, kv],
        out_shape=[
            jax.ShapeDtypeStruct((n, D_MODEL), F32),
            jax.ShapeDtypeStruct((n * N_HEADS, V_DIM), F32),
            jax.ShapeDtypeStruct((n * N_HEADS, V_DIM), F32),
        ],
        scratch_shapes=[
            pltpu.VMEM((tm, D_MODEL), BF16),
            pltpu.VMEM((tm, ATTN_WIDTH), F32),
            pltpu.VMEM((tm, ATTN_WIDTH), BF16),
            pltpu.VMEM((tm, ATTN_WIDTH), BF16),
            pltpu.VMEM((tm, D_MODEL), BF16),
            pltpu.VMEM((n_units, 2 * SEQ, SEQ), F32),
            pltpu.VMEM((n_units, SEQ, SEQ), BF16),
            pltpu.VMEM((n_units, SEQ, V_DIM), F32),
            pltpu.VMEM((tm, 3 * CONV_WIDTH), F32),
            pltpu.VMEM((tm, D_MODEL), F32),
        ],
        compiler_params=pltpu.CompilerParams(vmem_limit_bytes=VMEM_LIMIT),
        name="mixer_ctx",
    )(x2d, mod3, gpre, gpost, win, convw, lamqk, subg, wo)


def _rope(xh, cos, sin_lo, sin_hi):
    return xh * cos + pltpu.roll(xh, 16, 1) * sin_hi + pltpu.roll(xh, V_DIM - 16, 1) * sin_lo


def _rope_tables():
    t = jnp.arange(DEC_SEQ, dtype=jnp.int32)
    row = (t // GRID_W).astype(F32)
    col = (t % GRID_W).astype(F32)
    half = HEAD_DIM // 2
    freqs = ROPE_BASE ** (-jnp.arange(0, half, 2, dtype=F32) / half)
    ang = jnp.concatenate([row[:, None] * freqs, row[:, None] * freqs,
                           col[:, None] * freqs, col[:, None] * freqs], axis=-1)
    cos, sin = jnp.cos(ang), jnp.sin(ang)
    lane = jnp.arange(HEAD_DIM) % half
    first = (lane < half // 2)[None, :]
    sin_lo = jnp.where(first, -sin, 0.0)
    sin_hi = jnp.where(first, 0.0, sin)
    tile = lambda a: jnp.concatenate([a, a], axis=-1)
    return tile(cos), tile(sin_lo), tile(sin_hi)


def _lat_proj_kernel(x_ref, mod_ref, gpre_ref, win_ref, convw_ref, ck_ref, cv_ref, cos_ref, slo_ref, shi_ref,
                     qs_ref, kall_ref, vall_ref, conv_ref, u_ref):
    A, T, L = ATTN_WIDTH, DEC_SEQ, PAST_LEN
    u_ref[...] = _pre_mod(x_ref[...], mod_ref, gpre_ref, 1).astype(BF16)
    cos, slo, shi = cos_ref[...], slo_ref[...], shi_ref[...]
    q = _dot(u_ref[...], win_ref[:, 0:A].astype(BF16))
    for h in range(N_HEADS):
        cols = slice(h * V_DIM, (h + 1) * V_DIM)
        q0, q1 = _split_maps(_rope(q[:, cols], cos, slo, shi) * QK_SCALE_LOG2)
        for qb in range(T // LAT_QB):
            rows = slice(qb * LAT_QB, (qb + 1) * LAT_QB)
            qs_ref[0, h, qb, 0:LAT_QB, :] = q0[rows]
            qs_ref[0, h, qb, LAT_QB:, :] = q1[rows]
    k = _dot(u_ref[...], win_ref[:, A:2 * A].astype(BF16))
    for h in range(N_HEADS):
        cols = slice(h * V_DIM, (h + 1) * V_DIM)
        kall_ref[0, h, 0:L, :] = ck_ref[0, :, cols].astype(BF16)
        kall_ref[0, h, L:, :] = _rope(k[:, cols], cos, slo, shi).astype(BF16)
    v = _dot(u_ref[...], win_ref[:, 2 * A:3 * A].astype(BF16))
    for h in range(N_HEADS):
        cols = slice(h * V_DIM, (h + 1) * V_DIM)
        vall_ref[0, h, 0:L, :] = cv_ref[0, :, cols].astype(BF16)
        vall_ref[0, h, L:, :] = v[:, cols].astype(BF16)
    ub = u_ref[...]
    bg = _dot(ub, win_ref[:, 3 * A:3 * A + CONV_WIDTH].astype(BF16))
    cg = _dot(ub, win_ref[:, 3 * A + CONV_WIDTH:3 * A + 2 * CONV_WIDTH].astype(BF16))
    xc = _dot(ub, win_ref[:, 3 * A + 2 * CONV_WIDTH:].astype(BF16))
    conv_ref[...] = _short_conv(bg, cg, xc, convw_ref, T).astype(BF16)


def _lat_proj(x2d, mod3, gpre, win, convw, ck, cv):
    n = x2d.shape[0]
    T, L = DEC_SEQ, PAST_LEN
    nb = n // T
    nqb = T // LAT_QB
    cos, slo, shi = _rope_tables()
    cache = pl.BlockSpec((1, L, ATTN_WIDTH), lambda b: (b, 0, 0))
    return pl.pallas_call(
        _lat_proj_kernel,
        grid=(nb,),
        in_specs=[
            pl.BlockSpec((T, D_MODEL), lambda b: (b, 0)),
            pl.BlockSpec((1, 3 * N_SUB, D_MODEL), lambda b: (1 + b, 0, 0)),
            _const_spec((N_SUB, D_MODEL)),
            _const_spec((D_MODEL, IN_WIDTH)),
            _const_spec((3, CONV_WIDTH)),
            cache, cache,
            _const_spec((T, V_DIM)), _const_spec((T, V_DIM)), _const_spec((T, V_DIM)),
        ],
        out_specs=[
            pl.BlockSpec((1, N_HEADS, nqb, 2 * LAT_QB, V_DIM), lambda b: (b, 0, 0, 0, 0)),
            pl.BlockSpec((1, N_HEADS, L + T, V_DIM), lambda b: (b, 0, 0, 0)),
            pl.BlockSpec((1, N_HEADS, L + T, V_DIM), lambda b: (b, 0, 0, 0)),
            pl.BlockSpec((T, CONV_WIDTH), lambda b: (b, 0)),
        ],
        out_shape=[
            jax.ShapeDtypeStruct((nb, N_HEADS, nqb, 2 * LAT_QB, V_DIM), BF16),
            jax.ShapeDtypeStruct((nb, N_HEADS, L + T, V_DIM), BF16),
            jax.ShapeDtypeStruct((nb, N_HEADS, L + T, V_DIM), BF16),
            jax.ShapeDtypeStruct((n, CONV_WIDTH), BF16),
        ],
        scratch_shapes=[pltpu.VMEM((T, D_MODEL), BF16)],
        compiler_params=pltpu.CompilerParams(vmem_limit_bytes=VMEM_LIMIT),
        name="mixer_lat_proj",
    )(x2d, mod3, gpre, win, convw, ck, cv, cos, slo, shi)


def _lat_attn_kernel(x_ref, mod_ref, gpost_ref, qs_ref, kall_ref, vall_ref, conv_ref, lam_ref, subg_ref, wo_ref,
                     o_ref, cat_ref, s_ref, pd_ref, rz_ref):
    A = ATTN_WIDTH
    lam = _lambda(lam_ref)
    subg = subg_ref[...]

    def scores(h):
        _attn_scores(s_ref, h, qs_ref[0, h, 0], kall_ref[0, h])

    def probs(h):
        _attn_probs(s_ref, pd_ref, rz_ref, h, lam)

    def values(h):
        cat_ref[:, h * V_DIM:(h + 1) * V_DIM] = _attn_values(pd_ref, rz_ref, h, vall_ref[0, h], subg)

    scores(0)
    scores(1)
    mix_conv = _dot(conv_ref[...], wo_ref[A:, :].astype(BF16))
    for h in range(N_HEADS):
        probs(h)
        if h + 2 < N_HEADS:
            scores(h + 2)
        if h >= 1:
            values(h - 1)
    values(N_HEADS - 1)
    mix = mix_conv + _dot(cat_ref[...], wo_ref[:A, :].astype(BF16))
    gt = mod_ref[0, 5:6, :]
    o_ref[...] = x_ref[...] + gt * _rms(mix, gpost_ref[1:2, :])


def _lat_attn(x2d, mod3, gpost, qs, kall, vall, conv, lamqk, subg, wo):
    n = x2d.shape[0]
    T, L = DEC_SEQ, PAST_LEN
    nb = n // T
    nqb = T // LAT_QB
    row = pl.BlockSpec((LAT_QB, D_MODEL), lambda b, i: (b * nqb + i, 0))
    keys = pl.BlockSpec((1, N_HEADS, L + T, V_DIM), lambda b, i: (b, 0, 0, 0))
    return pl.pallas_call(
        _lat_attn_kernel,
        grid=(nb, nqb),
        in_specs=[
            row,
            pl.BlockSpec((1, 3 * N_SUB, D_MODEL), lambda b, i: (1 + b, 0, 0)),
            _const_spec((N_SUB, D_MODEL)),
            pl.BlockSpec((1, N_HEADS, 1, 2 * LAT_QB, V_DIM), lambda b, i: (b, 0, i, 0, 0)),
            keys, keys,
            pl.BlockSpec((LAT_QB, CONV_WIDTH), lambda b, i: (b * nqb + i, 0)),
            _const_spec((4, HEAD_DIM)),
            _const_spec((1, V_DIM)),
            _const_spec((D_MODEL, D_MODEL)),
        ],
        out_specs=row,
        out_shape=jax.ShapeDtypeStruct((n, D_MODEL), F32),
        scratch_shapes=[
            pltpu.VMEM((LAT_QB, ATTN_WIDTH), BF16),
            pltpu.VMEM((N_HEADS, 2 * LAT_QB, L + T), F32),
            pltpu.VMEM((N_HEADS, LAT_QB, L + T), BF16),
            pltpu.VMEM((N_HEADS, LAT_QB, V_DIM), F32),
        ],
        compiler_params=pltpu.CompilerParams(vmem_limit_bytes=VMEM_LIMIT),
        name="mixer_lat_attn",
    )(x2d, mod3, gpost, qs, kall, vall, conv, lamqk, subg, wo)


def kernel(x_prompt, x_sample, c, cache_k, cache_v, c_ctx, w_mod, b_mod, norm_pre, norm_post,
           ffn1_up, ffn1_down, ffn2_up, ffn2_down, w_in, conv_w, lam_qk, subln_g, w_o):
    batch, seq, _ = x_prompt.shape
    dec_batch, dec_seq, _ = x_sample.shape
    assert (seq, dec_seq) == (SEQ, DEC_SEQ) and cache_k.shape[1] == 1

    cond = jnp.zeros((COND_ROWS, D_MODEL), F32).at[0].set(c_ctx).at[1:1 + dec_batch].set(c)
    mod = _modulation(cond, w_mod[0], b_mod).reshape(COND_ROWS, 3 * N_SUB, D_MODEL)

    gpre, gpost = norm_pre[0], norm_post[0]
    up1, dn1, up2, dn2 = ffn1_up[0], ffn1_down[0], ffn2_up[0], ffn2_down[0]
    win, wo = w_in[0], w_o[0]
    convw, lamqk, subg = conv_w[0], lam_qk[0], subln_g

    xp = x_prompt.reshape(batch * seq, D_MODEL)
    xs = x_sample.reshape(dec_batch * dec_seq, D_MODEL)
    ck = cache_k[:, 0].reshape(dec_batch, PAST_LEN, ATTN_WIDTH)
    cv = cache_v[:, 0].reshape(dec_batch, PAST_LEN, ATTN_WIDTH)

    xp, xs = _ffn(xp, xs, mod, gpre, gpost, up1, dn1, sub=0)
    xp, new_k, new_v = _mix_ctx(xp, mod, gpre, gpost, win, convw, lamqk, subg, wo)
    qs, kall, vall, conv = _lat_proj(xs, mod, gpre, win, convw, ck, cv)
    xs = _lat_attn(xs, mod, gpost, qs, kall, vall, conv, lamqk, subg, wo)
    xp, xs = _ffn(xp, xs, mod, gpre, gpost, up2, dn2, sub=2)

    return (xp.reshape(batch, seq, D_MODEL), xs.reshape(dec_batch, dec_seq, D_MODEL),
            new_k.reshape(batch, 1, seq, N_HEADS, 2 * HEAD_DIM), new_v.reshape(batch, 1, seq, N_HEADS, V_DIM))
```

```python
import functools
import math

import jax
import jax.numpy as jnp
import numpy as np
from jax import lax
from jax.experimental import pallas as pl
from jax.experimental.pallas import tpu as pltpu

D_MODEL = 1024
SEQ = 256
DEC_SEQ = 1024
PAST_LEN = 512
GRID_W = 64
ATTN_WIDTH = 512
CONV_WIDTH = 512
N_HEADS = 4
HEAD_DIM = 64
V_DIM = 128
D_FF = 2816
ROPE_BASE = 10000.0
EPS = 1e-6
N_SUB = 3
IN_WIDTH = 3 * ATTN_WIDTH + 3 * CONV_WIDTH
LAMBDA_INIT = 0.8 - 0.6 * math.exp(-0.3 * 0)
QK_SCALE_LOG2 = HEAD_DIM ** -0.5 * math.log2(math.e)

F32 = jnp.float32
BF16 = jnp.bfloat16

MXU_TILE = 256
COND_ROWS = 8
FFN_STEP_ROWS = 1024
FFN_SUB = 512
FFN_CHUNKS = ((0, 1024), (1024, 2048), (2048, D_FF))
W_PIECE = MXU_TILE
N_PIECES = D_FF // W_PIECE
CTX_TM = 512
LAT_QB = 256
VMEM_LIMIT = 60 * 1024 * 1024


def _rms(x, g):
    ms = jnp.mean(x * x, axis=-1, keepdims=True)
    return x * lax.rsqrt(ms + EPS) * g


def _dot(a, b):
    return jnp.dot(a, b, preferred_element_type=F32)


def _dot_nt(a, b):
    return lax.dot_general(a, b, (((1,), (1,)), ((), ())), preferred_element_type=F32)


def _const_spec(shape):
    nd = len(shape)
    return pl.BlockSpec(shape, lambda *_: (0,) * nd, pipeline_mode=pl.Buffered(1))


def _mod_kernel(cctx_ref, c_ref, w_ref, b_ref, o_ref, cond_ref):
    nlat = c_ref.shape[0]
    cond_ref[...] = jnp.zeros_like(cond_ref)
    cond_ref[0:1, :] = cctx_ref[...]
    cond_ref[1:1 + nlat, :] = c_ref[...]
    c = cond_ref[...]
    s = (c * jax.nn.sigmoid(c)).astype(BF16)
    o_ref[0] = _dot(s, w_ref[...].astype(BF16)) + b_ref[...]


def _modulation(c_ctx, c, w_mod, b_mod):
    n_out = w_mod.shape[1]
    assert n_out == 3 * N_SUB * D_MODEL and 1 + c.shape[0] <= COND_ROWS
    return pl.pallas_call(
        _mod_kernel,
        grid=(3 * N_SUB,),
        in_specs=[
            pl.BlockSpec((1, D_MODEL), lambda j: (0, 0)),
            pl.BlockSpec(c.shape, lambda j: (0, 0)),
            pl.BlockSpec((D_MODEL, D_MODEL), lambda j: (0, j)),
            pl.BlockSpec((1, D_MODEL), lambda j: (0, j)),
        ],
        out_specs=pl.BlockSpec((1, COND_ROWS, D_MODEL), lambda j: (j, 0, 0)),
        out_shape=jax.ShapeDtypeStruct((3 * N_SUB, COND_ROWS, D_MODEL), F32),
        scratch_shapes=[pltpu.VMEM((COND_ROWS, D_MODEL), F32)],
        name="modulation",
    )(c_ctx, c, w_mod, b_mod)


def _mod_spec():
    return _const_spec((3 * N_SUB, COND_ROWS, D_MODEL))


def _mod_rows(mod_ref, sub, row):
    return tuple(mod_ref[3 * sub + i, pl.ds(row, 1), :] for i in range(3))


def _ffn_kernel(mod_ref, gpre_ref, gpost_ref, xp_hbm, xs_hbm, wup_hbm, wdn_hbm, op_hbm, os_hbm,
                wup_ref, wdn_ref, sa_ref, sb_ref, sd_ref, wsem, xbuf, xsem, obuf, osem, u_ref, acc_ref,
                *, sub, n_ctx, n_lat):
    t = pl.program_id(0)
    n = n_ctx + n_lat
    slot = t % 2
    R, S = FFN_STEP_ROWS, FFN_SUB
    n_sub = R // S

    def tile_dma(k, k_slot, action, into_vmem):
        def go(hbm_ref, blk):
            rows = hbm_ref.at[pl.ds(blk * R, R), :]
            if into_vmem:
                cp = pltpu.make_async_copy(rows, xbuf.at[k_slot], xsem.at[k_slot])
            else:
                cp = pltpu.make_async_copy(obuf.at[k_slot], rows, osem.at[k_slot])
            getattr(cp, action)()
        ctx_ref, lat_ref = (xp_hbm, xs_hbm) if into_vmem else (op_hbm, os_hbm)
        pl.when(k < n_ctx)(lambda: go(ctx_ref, k))
        pl.when(k >= n_ctx)(lambda: go(lat_ref, k - n_ctx))

    def weight_piece(j):
        ps = j % 2
        c = pl.ds(pl.multiple_of(j * W_PIECE, W_PIECE), W_PIECE)
        c2 = pl.ds(pl.multiple_of(D_FF + j * W_PIECE, W_PIECE), W_PIECE)
        return (
            pltpu.make_async_copy(wup_hbm.at[:, c], sa_ref.at[ps], wsem.at[ps, 0]),
            pltpu.make_async_copy(wup_hbm.at[:, c2], sb_ref.at[ps], wsem.at[ps, 1]),
            pltpu.make_async_copy(wdn_hbm.at[c, :], sd_ref.at[ps], wsem.at[ps, 2]),
        )

    def start_piece(j):
        for cp in weight_piece(j):
            cp.start()

    def convert_piece(j, carry):
        for cp in weight_piece(j):
            cp.wait()
        ps = j % 2
        wup_ref[0, j] = sa_ref[ps].astype(BF16)
        wup_ref[1, j] = sb_ref[ps].astype(BF16)
        wdn_ref[j] = sd_ref[ps].astype(BF16)
        pl.when(j + 2 < N_PIECES)(lambda: start_piece(j + 2))
        return carry

    sh, sc, gt = _mod_rows(mod_ref, sub, jnp.where(t < n_ctx, 0, 1 + t - n_ctx))

    def pre(s):
        x = xbuf[slot, s * S:(s + 1) * S, :]
        u_ref[s] = (_rms(x, gpre_ref[sub:sub + 1, :]) * (1.0 + sc) + sh).astype(BF16)

    def mm(s, ci):
        lo, hi = FFN_CHUNKS[ci]
        pieces = range(lo // W_PIECE, hi // W_PIECE)
        ub = u_ref[s]
        a = jnp.concatenate([_dot(ub, wup_ref[0, j]) for j in pieces], axis=1)
        b = jnp.concatenate([_dot(ub, wup_ref[1, j]) for j in pieces], axis=1)
        h = (a * jax.nn.sigmoid(a) * b).astype(BF16)
        part = _dot(h, wdn_ref[pieces.start:pieces.stop].reshape(hi - lo, D_MODEL))
        if ci == 0:
            acc_ref[s] = part
        else:
            acc_ref[s] += part

    def post(s):
        x = xbuf[slot, s * S:(s + 1) * S, :]
        obuf[slot, s * S:(s + 1) * S, :] = x + 0.5 * gt * _rms(acc_ref[s], gpost_ref[sub:sub + 1, :])

    @pl.when(t == 0)
    def _():
        tile_dma(t, slot, "start", True)
        start_piece(0)
        start_piece(1)

    tile_dma(t, slot, "wait", True)
    pl.when(t + 1 < n)(lambda: tile_dma(t + 1, 1 - slot, "start", True))
    pl.when(t >= 2)(lambda: tile_dma(t - 2, slot, "wait", False))

    @pl.when(t == 0)
    def _():
        lax.fori_loop(0, N_PIECES, convert_piece, 0)

    for s in range(n_sub):
        pre(s)
    for ci in range(len(FFN_CHUNKS)):
        for s in range(n_sub):
            mm(s, ci)
    for s in range(n_sub):
        post(s)

    tile_dma(t, slot, "start", False)

    @pl.when(t == n - 1)
    def _():
        if n >= 2:
            tile_dma(t - 1, 1 - slot, "wait", False)
        tile_dma(t, slot, "wait", False)


def _ffn(xp2d, xs2d, mod3, gpre, gpost, wup, wdn, *, sub):
    R = FFN_STEP_ROWS
    n_ctx, n_lat = xp2d.shape[0] // R, xs2d.shape[0] // R
    assert R == DEC_SEQ and xp2d.shape[0] % R == 0 and xs2d.shape[0] % R == 0
    any_spec = pl.BlockSpec(memory_space=pl.ANY)
    return pl.pallas_call(
        functools.partial(_ffn_kernel, sub=sub, n_ctx=n_ctx, n_lat=n_lat),
        grid=(n_ctx + n_lat,),
        in_specs=[
            _mod_spec(),
            _const_spec((N_SUB, D_MODEL)),
            _const_spec((N_SUB, D_MODEL)),
            any_spec, any_spec, any_spec, any_spec,
        ],
        out_specs=[any_spec, any_spec],
        out_shape=[jax.ShapeDtypeStruct(xp2d.shape, F32), jax.ShapeDtypeStruct(xs2d.shape, F32)],
        scratch_shapes=[
            pltpu.VMEM((2, N_PIECES, D_MODEL, W_PIECE), BF16),
            pltpu.VMEM((N_PIECES, W_PIECE, D_MODEL), BF16),
            pltpu.VMEM((2, D_MODEL, W_PIECE), F32),
            pltpu.VMEM((2, D_MODEL, W_PIECE), F32),
            pltpu.VMEM((2, W_PIECE, D_MODEL), F32),
            pltpu.SemaphoreType.DMA((2, 3)),
            pltpu.VMEM((2, R, D_MODEL), F32),
            pltpu.SemaphoreType.DMA((2,)),
            pltpu.VMEM((2, R, D_MODEL), F32),
            pltpu.SemaphoreType.DMA((2,)),
            pltpu.VMEM((R // FFN_SUB, FFN_SUB, D_MODEL), BF16),
            pltpu.VMEM((R // FFN_SUB, FFN_SUB, D_MODEL), F32),
        ],
        compiler_params=pltpu.CompilerParams(dimension_semantics=("arbitrary",), vmem_limit_bytes=VMEM_LIMIT),
        name=f"ffn{sub}",
    )(mod3, gpre, gpost, xp2d, xs2d, wup, wdn)


def _lambda(lam_ref):
    lq = lam_ref[...]
    l1 = jnp.sum(lq[0:1, :] * lq[1:2, :], axis=-1, keepdims=True)
    l2 = jnp.sum(lq[2:3, :] * lq[3:4, :], axis=-1, keepdims=True)
    return jnp.exp(l1) - jnp.exp(l2) + LAMBDA_INIT


def _split_maps(qh):
    lane = lax.broadcasted_iota(jnp.int32, qh.shape, 1)
    q0 = jnp.where(lane < HEAD_DIM, qh, 0.0).astype(BF16)
    q1 = jnp.where(lane >= HEAD_DIM, qh, 0.0).astype(BF16)
    return q0, q1


def _attn_scores(s_ref, idx, qs, kh):
    s_ref[idx] = _dot_nt(qs, kh)


def _attn_probs(s_ref, pd_ref, rz_ref, idx, lam):
    s = s_ref[idx]
    m = s.shape[0] // 2
    e = jnp.exp2(s - jnp.max(s, axis=-1, keepdims=True))
    z = jnp.sum(e, axis=-1, keepdims=True)
    c = lam * z[:m] / z[m:]
    pd_ref[idx] = (e[:m] - c * e[m:]).astype(BF16)
    rz_ref[idx] = jnp.broadcast_to(1.0 / z[:m], (m, V_DIM))


def _attn_values(pd_ref, rz_ref, idx, vh, subg):
    o = _dot(pd_ref[idx], vh) * rz_ref[idx]
    return (_rms(o, subg) * (1.0 - LAMBDA_INIT)).astype(BF16)


def _short_conv(bg, cg, xc, convw_ref, seq):
    g = cg * xc
    m = g.shape[0]
    row = lax.broadcasted_iota(jnp.int32, g.shape, 0) % seq
    g_prev = jnp.where(row == 0, 0.0, pltpu.roll(g, 1, 0))
    g_next = jnp.where(row == seq - 1, 0.0, pltpu.roll(g, m - 1, 0))
    conv = g_prev * convw_ref[0:1, :] + g * convw_ref[1:2, :] + g_next * convw_ref[2:3, :]
    return bg * conv


def _pre_mod(x, mod_ref, gpre_ref, sub, row):
    sh, sc, _ = _mod_rows(mod_ref, sub, row)
    return _rms(x, gpre_ref[sub:sub + 1, :]) * (1.0 + sc) + sh


def _mix_ctx_kernel(x_ref, mod_ref, gpre_ref, gpost_ref, win_ref, convw_ref, lam_ref, subg_ref, wo_ref,
                    o_ref, k_ref, v_ref, u_ref, q_ref, kb_ref, vb_ref, cat_ref, s_ref, pd_ref, rz_ref, cv_ref,
                    mixc_ref):
    A = ATTN_WIDTH
    x = x_ref[...]
    u_ref[...] = _pre_mod(x, mod_ref, gpre_ref, 1, 0).astype(BF16)
    q_ref[...] = _dot(u_ref[...], win_ref[:, 0:A].astype(BF16)) * QK_SCALE_LOG2
    k = _dot(u_ref[...], win_ref[:, A:2 * A].astype(BF16))
    v = _dot(u_ref[...], win_ref[:, 2 * A:3 * A].astype(BF16))
    kb_ref[...] = k.astype(BF16)
    vb_ref[...] = v.astype(BF16)
    for h in range(N_HEADS):
        k_ref[pl.ds(h, CTX_TM, stride=N_HEADS), :] = k[:, h * V_DIM:(h + 1) * V_DIM]
        v_ref[pl.ds(h, CTX_TM, stride=N_HEADS), :] = v[:, h * V_DIM:(h + 1) * V_DIM]
    lam = _lambda(lam_ref)
    subg = subg_ref[...]
    n_units = (CTX_TM // SEQ) * N_HEADS
    n_conv_pieces = 3 * CONV_WIDTH // MXU_TILE

    def unit_slices(uid):
        b, h = divmod(uid, N_HEADS)
        return slice(b * SEQ, (b + 1) * SEQ), slice(h * V_DIM, (h + 1) * V_DIM)

    def conv_piece(i):
        c0 = 3 * A + i * MXU_TILE
        cv_ref[:, i * MXU_TILE:(i + 1) * MXU_TILE] = _dot(u_ref[...], win_ref[:, c0:c0 + MXU_TILE].astype(BF16))

    def mix_conv_piece(j):
        cols = slice(j * MXU_TILE, (j + 1) * MXU_TILE)
        mixc_ref[:, cols] = _dot(cat_ref[:, A:], wo_ref[A:, cols].astype(BF16))

    for uid in range(n_units):
        rows, cols = unit_slices(uid)
        q0, q1 = _split_maps(q_ref[rows, cols])
        _attn_scores(s_ref, uid, jnp.concatenate([q0, q1], axis=0), kb_ref[rows, cols])
    for uid in range(n_units):
        _attn_probs(s_ref, pd_ref, rz_ref, uid, lam)
        if uid < n_conv_pieces:
            conv_piece(uid)
    bg = cv_ref[:, 0:CONV_WIDTH]
    cg = cv_ref[:, CONV_WIDTH:2 * CONV_WIDTH]
    xc = cv_ref[:, 2 * CONV_WIDTH:]
    cat_ref[:, A:] = _short_conv(bg, cg, xc, convw_ref, SEQ).astype(BF16)
    for uid in range(n_units):
        rows, cols = unit_slices(uid)
        cat_ref[rows, cols] = _attn_values(pd_ref, rz_ref, uid, vb_ref[rows, cols], subg)
        if uid % 2 == 1:
            mix_conv_piece(uid // 2)
    mix = mixc_ref[...] + _dot(cat_ref[:, :A], wo_ref[:A, :].astype(BF16))
    gt = _mod_rows(mod_ref, 1, 0)[2]
    o_ref[...] = x + gt * _rms(mix, gpost_ref[1:2, :])


def _mix_ctx(x2d, mod3, gpre, gpost, win, convw, lamqk, subg, wo):
    n = x2d.shape[0]
    tm = CTX_TM
    n_units = (tm // SEQ) * N_HEADS
    assert 3 * CONV_WIDTH // MXU_TILE <= n_units and D_MODEL // MXU_TILE == n_units // 2
    row = pl.BlockSpec((tm, D_MODEL), lambda t: (t, 0))
    kv = pl.BlockSpec((tm * N_HEADS, V_DIM), lambda t: (t, 0))
    return pl.pallas_call(
        _mix_ctx_kernel,
        grid=(n // tm,),
        in_specs=[
            row,
            _mod_spec(),
            _const_spec((N_SUB, D_MODEL)),
            _const_spec((N_SUB, D_MODEL)),
            _const_spec((D_MODEL, IN_WIDTH)),
            _const_spec((3, CONV_WIDTH)),
            _const_spec((4, HEAD_DIM)),
            _const_spec((1, V_DIM)),
            _const_spec((D_MODEL, D_MODEL)),
        ],
        out_specs=[row, kv, kv],
        out_shape=[
            jax.ShapeDtypeStruct((n, D_MODEL), F32),
            jax.ShapeDtypeStruct((n * N_HEADS, V_DIM), F32),
            jax.ShapeDtypeStruct((n * N_HEADS, V_DIM), F32),
        ],
        scratch_shapes=[
            pltpu.VMEM((tm, D_MODEL), BF16),
            pltpu.VMEM((tm, ATTN_WIDTH), F32),
            pltpu.VMEM((tm, ATTN_WIDTH), BF16),
            pltpu.VMEM((tm, ATTN_WIDTH), BF16),
            pltpu.VMEM((tm, D_MODEL), BF16),
            pltpu.VMEM((n_units, 2 * SEQ, SEQ), F32),
            pltpu.VMEM((n_units, SEQ, SEQ), BF16),
            pltpu.VMEM((n_units, SEQ, V_DIM), F32),
            pltpu.VMEM((tm, 3 * CONV_WIDTH), F32),
            pltpu.VMEM((tm, D_MODEL), F32),
        ],
        compiler_params=pltpu.CompilerParams(vmem_limit_bytes=VMEM_LIMIT),
        name="mixer_ctx",
    )(x2d, mod3, gpre, gpost, win, convw, lamqk, subg, wo)


def _rope(xh, cos, sin_lo, sin_hi):
    return xh * cos + pltpu.roll(xh, 16, 1) * sin_hi + pltpu.roll(xh, V_DIM - 16, 1) * sin_lo


def _rope_tables():
    t = np.arange(DEC_SEQ)
    row = (t // GRID_W).astype(np.float64)
    col = (t % GRID_W).astype(np.float64)
    half = HEAD_DIM // 2
    freqs = ROPE_BASE ** (-np.arange(0, half, 2, dtype=np.float64) / half)
    ang = np.concatenate([row[:, None] * freqs, row[:, None] * freqs,
                          col[:, None] * freqs, col[:, None] * freqs], axis=-1)
    cos, sin = np.cos(ang), np.sin(ang)
    first = ((np.arange(HEAD_DIM) % half) < half // 2)[None, :]
    sin_lo = np.where(first, -sin, 0.0)
    sin_hi = np.where(first, 0.0, sin)
    tile = lambda a: jnp.asarray(np.concatenate([a, a], axis=-1), dtype=F32)
    return tile(cos), tile(sin_lo), tile(sin_hi)


def _lat_proj_kernel(x_ref, mod_ref, gpre_ref, win_ref, convw_ref, ck_ref, cv_ref, cos_ref, slo_ref, shi_ref,
                     qs_ref, kall_ref, vall_ref, conv_ref, u_ref):
    A, T, L = ATTN_WIDTH, DEC_SEQ, PAST_LEN
    u_ref[...] = _pre_mod(x_ref[...], mod_ref, gpre_ref, 1, 1 + pl.program_id(0)).astype(BF16)
    cos, slo, shi = cos_ref[...], slo_ref[...], shi_ref[...]
    q = _dot(u_ref[...], win_ref[:, 0:A].astype(BF16))
    for h in range(N_HEADS):
        cols = slice(h * V_DIM, (h + 1) * V_DIM)
        q0, q1 = _split_maps(_rope(q[:, cols], cos, slo, shi) * QK_SCALE_LOG2)
        for qb in range(T // LAT_QB):
            rows = slice(qb * LAT_QB, (qb + 1) * LAT_QB)
            qs_ref[0, h, qb, 0:LAT_QB, :] = q0[rows]
            qs_ref[0, h, qb, LAT_QB:, :] = q1[rows]
    k = _dot(u_ref[...], win_ref[:, A:2 * A].astype(BF16))
    for h in range(N_HEADS):
        cols = slice(h * V_DIM, (h + 1) * V_DIM)
        kall_ref[0, h, 0:L, :] = ck_ref[0, pl.ds(h, L, stride=N_HEADS), :].astype(BF16)
        kall_ref[0, h, L:, :] = _rope(k[:, cols], cos, slo, shi).astype(BF16)
    v = _dot(u_ref[...], win_ref[:, 2 * A:3 * A].astype(BF16))
    for h in range(N_HEADS):
        cols = slice(h * V_DIM, (h + 1) * V_DIM)
        vall_ref[0, h, 0:L, :] = cv_ref[0, pl.ds(h, L, stride=N_HEADS), :].astype(BF16)
        vall_ref[0, h, L:, :] = v[:, cols].astype(BF16)
    ub = u_ref[...]
    bg = _dot(ub, win_ref[:, 3 * A:3 * A + CONV_WIDTH].astype(BF16))
    cg = _dot(ub, win_ref[:, 3 * A + CONV_WIDTH:3 * A + 2 * CONV_WIDTH].astype(BF16))
    xc = _dot(ub, win_ref[:, 3 * A + 2 * CONV_WIDTH:].astype(BF16))
    conv_ref[...] = _short_conv(bg, cg, xc, convw_ref, T).astype(BF16)


def _lat_proj(x2d, mod3, gpre, win, convw, ck, cv):
    n = x2d.shape[0]
    T, L = DEC_SEQ, PAST_LEN
    nb = n // T
    nqb = T // LAT_QB
    cos, slo, shi = _rope_tables()
    cache = pl.BlockSpec((1, L * N_HEADS, V_DIM), lambda b: (b, 0, 0))
    return pl.pallas_call(
        _lat_proj_kernel,
        grid=(nb,),
        in_specs=[
            pl.BlockSpec((T, D_MODEL), lambda b: (b, 0)),
            _mod_spec(),
            _const_spec((N_SUB, D_MODEL)),
            _const_spec((D_MODEL, IN_WIDTH)),
            _const_spec((3, CONV_WIDTH)),
            cache, cache,
            _const_spec((T, V_DIM)), _const_spec((T, V_DIM)), _const_spec((T, V_DIM)),
        ],
        out_specs=[
            pl.BlockSpec((1, N_HEADS, nqb, 2 * LAT_QB, V_DIM), lambda b: (b, 0, 0, 0, 0)),
            pl.BlockSpec((1, N_HEADS, L + T, V_DIM), lambda b: (b, 0, 0, 0)),
            pl.BlockSpec((1, N_HEADS, L + T, V_DIM), lambda b: (b, 0, 0, 0)),
            pl.BlockSpec((T, CONV_WIDTH), lambda b: (b, 0)),
        ],
        out_shape=[
            jax.ShapeDtypeStruct((nb, N_HEADS, nqb, 2 * LAT_QB, V_DIM), BF16),
            jax.ShapeDtypeStruct((nb, N_HEADS, L + T, V_DIM), BF16),
            jax.ShapeDtypeStruct((nb, N_HEADS, L + T, V_DIM), BF16),
            jax.ShapeDtypeStruct((n, CONV_WIDTH), BF16),
        ],
        scratch_shapes=[pltpu.VMEM((T, D_MODEL), BF16)],
        compiler_params=pltpu.CompilerParams(vmem_limit_bytes=VMEM_LIMIT),
        name="mixer_lat_proj",
    )(x2d, mod3, gpre, win, convw, ck, cv, cos, slo, shi)


def _lat_attn_kernel(x_ref, mod_ref, gpost_ref, qs_ref, kall_ref, vall_ref, conv_ref, lam_ref, subg_ref, wo_ref,
                     o_ref, cat_ref, s_ref, pd_ref, rz_ref):
    A = ATTN_WIDTH
    lam = _lambda(lam_ref)
    subg = subg_ref[...]

    def scores(h):
        _attn_scores(s_ref, h, qs_ref[0, h, 0], kall_ref[0, h])

    def probs(h):
        _attn_probs(s_ref, pd_ref, rz_ref, h, lam)

    def values(h):
        cat_ref[:, h * V_DIM:(h + 1) * V_DIM] = _attn_values(pd_ref, rz_ref, h, vall_ref[0, h], subg)

    scores(0)
    scores(1)
    mix_conv = _dot(conv_ref[...], wo_ref[A:, :].astype(BF16))
    for h in range(N_HEADS):
        probs(h)
        if h + 2 < N_HEADS:
            scores(h + 2)
        if h >= 1:
            values(h - 1)
    values(N_HEADS - 1)
    mix = mix_conv + _dot(cat_ref[...], wo_ref[:A, :].astype(BF16))
    gt = _mod_rows(mod_ref, 1, 1 + pl.program_id(0))[2]
    o_ref[...] = x_ref[...] + gt * _rms(mix, gpost_ref[1:2, :])


def _lat_attn(x2d, mod3, gpost, qs, kall, vall, conv, lamqk, subg, wo):
    n = x2d.shape[0]
    T, L = DEC_SEQ, PAST_LEN
    nb = n // T
    nqb = T // LAT_QB
    row = pl.BlockSpec((LAT_QB, D_MODEL), lambda b, i: (b * nqb + i, 0))
    keys = pl.BlockSpec((1, N_HEADS, L + T, V_DIM), lambda b, i: (b, 0, 0, 0))
    return pl.pallas_call(
        _lat_attn_kernel,
        grid=(nb, nqb),
        in_specs=[
            row,
            _mod_spec(),
            _const_spec((N_SUB, D_MODEL)),
            pl.BlockSpec((1, N_HEADS, 1, 2 * LAT_QB, V_DIM), lambda b, i: (b, 0, i, 0, 0)),
            keys, keys,
            pl.BlockSpec((LAT_QB, CONV_WIDTH), lambda b, i: (b * nqb + i, 0)),
            _const_spec((4, HEAD_DIM)),
            _const_spec((1, V_DIM)),
            _const_spec((D_MODEL, D_MODEL)),
        ],
        out_specs=row,
        out_shape=jax.ShapeDtypeStruct((n, D_MODEL), F32),
        scratch_shapes=[
            pltpu.VMEM((LAT_QB, ATTN_WIDTH), BF16),
            pltpu.VMEM((N_HEADS, 2 * LAT_QB, L + T), F32),
            pltpu.VMEM((N_HEADS, LAT_QB, L + T), BF16),
            pltpu.VMEM((N_HEADS, LAT_QB, V_DIM), F32),
        ],
        compiler_params=pltpu.CompilerParams(vmem_limit_bytes=VMEM_LIMIT),
        name="mixer_lat_attn",
    )(x2d, mod3, gpost, qs, kall, vall, conv, lamqk, subg, wo)


def kernel(x_prompt, x_sample, c, cache_k, cache_v, c_ctx, w_mod, b_mod, norm_pre, norm_post,
           ffn1_up, ffn1_down, ffn2_up, ffn2_down, w_in, conv_w, lam_qk, subln_g, w_o):
    batch, seq, _ = x_prompt.shape
    dec_batch, dec_seq, _ = x_sample.shape
    assert (seq, dec_seq) == (SEQ, DEC_SEQ) and cache_k.shape[1] == 1

    mod = _modulation(c_ctx.reshape(1, D_MODEL), c, w_mod[0], b_mod)

    gpre, gpost = norm_pre[0], norm_post[0]
    up1, dn1, up2, dn2 = ffn1_up[0], ffn1_down[0], ffn2_up[0], ffn2_down[0]
    win, wo = w_in[0], w_o[0]
    convw, lamqk, subg = conv_w[0], lam_qk[0], subln_g

    xp = x_prompt.reshape(batch * seq, D_MODEL)
    xs = x_sample.reshape(dec_batch * dec_seq, D_MODEL)
    ck = cache_k.reshape(dec_batch, PAST_LEN * N_HEADS, 2 * HEAD_DIM)
    cv = cache_v.reshape(dec_batch, PAST_LEN * N_HEADS, V_DIM)

    xp, xs = _ffn(xp, xs, mod, gpre, gpost, up1, dn1, sub=0)
    xp, new_k, new_v = _mix_ctx(xp, mod, gpre, gpost, win, convw, lamqk, subg, wo)
    qs, kall, vall, conv = _lat_proj(xs, mod, gpre, win, convw, ck, cv)
    xs = _lat_attn(xs, mod, gpost, qs, kall, vall, conv, lamqk, subg, wo)
    xp, xs = _ffn(xp, xs, mod, gpre, gpost, up2, dn2, sub=2)

    return (xp.reshape(batch, seq, D_MODEL), xs.reshape(dec_batch, dec_seq, D_MODEL),
            new_k.reshape(batch, 1, seq, N_HEADS, 2 * HEAD_DIM), new_v.reshape(batch, 1, seq, N_HEADS, V_DIM))
```

```python
import functools
import math

import jax
import jax.numpy as jnp
import numpy as np
from jax import lax
from jax.experimental import pallas as pl
from jax.experimental.pallas import tpu as pltpu

D_MODEL = 1024
SEQ = 256
DEC_SEQ = 1024
PAST_LEN = 512
GRID_W = 64
ATTN_WIDTH = 512
CONV_WIDTH = 512
N_HEADS = 4
HEAD_DIM = 64
V_DIM = 128
D_FF = 2816
ROPE_BASE = 10000.0
EPS = 1e-6
N_SUB = 3
IN_WIDTH = 3 * ATTN_WIDTH + 3 * CONV_WIDTH
LAMBDA_INIT = 0.8 - 0.6 * math.exp(-0.3 * 0)
QK_SCALE_LOG2 = HEAD_DIM ** -0.5 * math.log2(math.e)

F32 = jnp.float32
BF16 = jnp.bfloat16

MXU_TILE = 256
COND_ROWS = 8
FFN_STEP_ROWS = 1024
FFN_SUB = 512
FFN_CHUNKS = ((0, 1024), (1024, 2048), (2048, D_FF))
W_PIECE = MXU_TILE
N_PIECES = D_FF // W_PIECE
CTX_TM = 512
LAT_QB = 512
VMEM_LIMIT = 60 * 1024 * 1024


def _rms(x, g):
    ms = jnp.mean(x * x, axis=-1, keepdims=True)
    return x * lax.rsqrt(ms + EPS) * g


def _dot(a, b):
    return jnp.dot(a, b, preferred_element_type=F32)


def _dot_nt(a, b):
    return lax.dot_general(a, b, (((1,), (1,)), ((), ())), preferred_element_type=F32)


def _const_spec(shape):
    nd = len(shape)
    return pl.BlockSpec(shape, lambda *_: (0,) * nd, pipeline_mode=pl.Buffered(1))


def _mod_kernel(cctx_ref, c_ref, w_ref, b_ref, o_ref, cond_ref):
    nlat = c_ref.shape[0]
    cond_ref[...] = jnp.zeros_like(cond_ref)
    cond_ref[0:1, :] = cctx_ref[...]
    cond_ref[1:1 + nlat, :] = c_ref[...]
    c = cond_ref[...]
    s = (c * jax.nn.sigmoid(c)).astype(BF16)
    o_ref[0] = _dot(s, w_ref[...].astype(BF16)) + b_ref[...]


def _modulation(c_ctx, c, w_mod, b_mod):
    n_out = w_mod.shape[1]
    assert n_out == 3 * N_SUB * D_MODEL and 1 + c.shape[0] <= COND_ROWS
    return pl.pallas_call(
        _mod_kernel,
        grid=(3 * N_SUB,),
        in_specs=[
            pl.BlockSpec((1, D_MODEL), lambda j: (0, 0)),
            pl.BlockSpec(c.shape, lambda j: (0, 0)),
            pl.BlockSpec((D_MODEL, D_MODEL), lambda j: (0, j)),
            pl.BlockSpec((1, D_MODEL), lambda j: (0, j)),
        ],
        out_specs=pl.BlockSpec((1, COND_ROWS, D_MODEL), lambda j: (j, 0, 0)),
        out_shape=jax.ShapeDtypeStruct((3 * N_SUB, COND_ROWS, D_MODEL), F32),
        scratch_shapes=[pltpu.VMEM((COND_ROWS, D_MODEL), F32)],
        name="modulation",
    )(c_ctx, c, w_mod, b_mod)


def _mod_spec():
    return _const_spec((3 * N_SUB, COND_ROWS, D_MODEL))


def _mod_rows(mod_ref, sub, row):
    return tuple(mod_ref[3 * sub + i, pl.ds(row, 1), :] for i in range(3))


def _ffn_kernel(mod_ref, gpre_ref, gpost_ref, xp_hbm, xs_hbm, wup_hbm, wdn_hbm, op_hbm, os_hbm,
                wup_ref, wdn_ref, sa_ref, sb_ref, sd_ref, wsem, xbuf, xsem, obuf, osem, u_ref, acc_ref,
                *, sub, n_ctx, n_lat):
    t = pl.program_id(0)
    n = n_ctx + n_lat
    slot = t % 2
    R, S = FFN_STEP_ROWS, FFN_SUB
    n_sub = R // S

    def tile_dma(k, k_slot, action, into_vmem):
        def go(hbm_ref, blk):
            rows = hbm_ref.at[pl.ds(blk * R, R), :]
            if into_vmem:
                cp = pltpu.make_async_copy(rows, xbuf.at[k_slot], xsem.at[k_slot])
            else:
                cp = pltpu.make_async_copy(obuf.at[k_slot], rows, osem.at[k_slot])
            getattr(cp, action)()
        ctx_ref, lat_ref = (xp_hbm, xs_hbm) if into_vmem else (op_hbm, os_hbm)
        pl.when(k < n_ctx)(lambda: go(ctx_ref, k))
        pl.when(k >= n_ctx)(lambda: go(lat_ref, k - n_ctx))

    def weight_piece(j):
        ps = j % 2
        c = pl.ds(pl.multiple_of(j * W_PIECE, W_PIECE), W_PIECE)
        c2 = pl.ds(pl.multiple_of(D_FF + j * W_PIECE, W_PIECE), W_PIECE)
        return (
            pltpu.make_async_copy(wup_hbm.at[:, c], sa_ref.at[ps], wsem.at[ps, 0]),
            pltpu.make_async_copy(wup_hbm.at[:, c2], sb_ref.at[ps], wsem.at[ps, 1]),
            pltpu.make_async_copy(wdn_hbm.at[c, :], sd_ref.at[ps], wsem.at[ps, 2]),
        )

    def start_piece(j):
        for cp in weight_piece(j):
            cp.start()

    def convert_piece(j, carry):
        for cp in weight_piece(j):
            cp.wait()
        ps = j % 2
        wup_ref[0, j] = sa_ref[ps].astype(BF16)
        wup_ref[1, j] = sb_ref[ps].astype(BF16)
        wdn_ref[j] = sd_ref[ps].astype(BF16)
        pl.when(j + 2 < N_PIECES)(lambda: start_piece(j + 2))
        return carry

    sh, sc, gt = _mod_rows(mod_ref, sub, jnp.where(t < n_ctx, 0, 1 + t - n_ctx))

    def pre(s):
        x = xbuf[slot, s * S:(s + 1) * S, :]
        u_ref[s] = (_rms(x, gpre_ref[sub:sub + 1, :]) * (1.0 + sc) + sh).astype(BF16)

    def mm(s, ci):
        lo, hi = FFN_CHUNKS[ci]
        pieces = range(lo // W_PIECE, hi // W_PIECE)
        ub = u_ref[s]
        a = jnp.concatenate([_dot(ub, wup_ref[0, j]) for j in pieces], axis=1)
        b = jnp.concatenate([_dot(ub, wup_ref[1, j]) for j in pieces], axis=1)
        h = (a * jax.nn.sigmoid(a) * b).astype(BF16)
        part = _dot(h, wdn_ref[pieces.start:pieces.stop].reshape(hi - lo, D_MODEL))
        if ci == 0:
            acc_ref[s] = part
        else:
            acc_ref[s] += part

    def post(s):
        x = xbuf[slot, s * S:(s + 1) * S, :]
        obuf[slot, s * S:(s + 1) * S, :] = x + 0.5 * gt * _rms(acc_ref[s], gpost_ref[sub:sub + 1, :])

    @pl.when(t == 0)
    def _():
        tile_dma(t, slot, "start", True)
        start_piece(0)
        start_piece(1)

    tile_dma(t, slot, "wait", True)
    pl.when(t + 1 < n)(lambda: tile_dma(t + 1, 1 - slot, "start", True))
    pl.when(t >= 2)(lambda: tile_dma(t - 2, slot, "wait", False))

    @pl.when(t == 0)
    def _():
        lax.fori_loop(0, N_PIECES, convert_piece, 0)

    for s in range(n_sub):
        pre(s)
    for ci in range(len(FFN_CHUNKS)):
        for s in range(n_sub):
            mm(s, ci)
    for s in range(n_sub):
        post(s)

    tile_dma(t, slot, "start", False)

    @pl.when(t == n - 1)
    def _():
        if n >= 2:
            tile_dma(t - 1, 1 - slot, "wait", False)
        tile_dma(t, slot, "wait", False)


def _ffn(xp2d, xs2d, mod3, gpre, gpost, wup, wdn, *, sub):
    R = FFN_STEP_ROWS
    n_ctx, n_lat = xp2d.shape[0] // R, xs2d.shape[0] // R
    assert R == DEC_SEQ and xp2d.shape[0] % R == 0 and xs2d.shape[0] % R == 0
    any_spec = pl.BlockSpec(memory_space=pl.ANY)
    return pl.pallas_call(
        functools.partial(_ffn_kernel, sub=sub, n_ctx=n_ctx, n_lat=n_lat),
        grid=(n_ctx + n_lat,),
        in_specs=[
            _mod_spec(),
            _const_spec((N_SUB, D_MODEL)),
            _const_spec((N_SUB, D_MODEL)),
            any_spec, any_spec, any_spec, any_spec,
        ],
        out_specs=[any_spec, any_spec],
        out_shape=[jax.ShapeDtypeStruct(xp2d.shape, F32), jax.ShapeDtypeStruct(xs2d.shape, F32)],
        scratch_shapes=[
            pltpu.VMEM((2, N_PIECES, D_MODEL, W_PIECE), BF16),
            pltpu.VMEM((N_PIECES, W_PIECE, D_MODEL), BF16),
            pltpu.VMEM((2, D_MODEL, W_PIECE), F32),
            pltpu.VMEM((2, D_MODEL, W_PIECE), F32),
            pltpu.VMEM((2, W_PIECE, D_MODEL), F32),
            pltpu.SemaphoreType.DMA((2, 3)),
            pltpu.VMEM((2, R, D_MODEL), F32),
            pltpu.SemaphoreType.DMA((2,)),
            pltpu.VMEM((2, R, D_MODEL), F32),
            pltpu.SemaphoreType.DMA((2,)),
            pltpu.VMEM((R // FFN_SUB, FFN_SUB, D_MODEL), BF16),
            pltpu.VMEM((R // FFN_SUB, FFN_SUB, D_MODEL), F32),
        ],
        compiler_params=pltpu.CompilerParams(dimension_semantics=("arbitrary",), vmem_limit_bytes=VMEM_LIMIT),
        name=f"ffn{sub}",
    )(mod3, gpre, gpost, xp2d, xs2d, wup, wdn)


def _lambda(lam_ref):
    lq = lam_ref[...]
    l1 = jnp.sum(lq[0:1, :] * lq[1:2, :], axis=-1, keepdims=True)
    l2 = jnp.sum(lq[2:3, :] * lq[3:4, :], axis=-1, keepdims=True)
    return jnp.exp(l1) - jnp.exp(l2) + LAMBDA_INIT


def _split_maps(qh):
    lane = lax.broadcasted_iota(jnp.int32, qh.shape, 1)
    q0 = jnp.where(lane < HEAD_DIM, qh, 0.0).astype(BF16)
    q1 = jnp.where(lane >= HEAD_DIM, qh, 0.0).astype(BF16)
    return q0, q1


def _attn_scores(s_ref, idx, qs, kh):
    s_ref[idx] = _dot_nt(qs, kh)


def _attn_probs(s_ref, pd_ref, rz_ref, idx, lam):
    s = s_ref[idx]
    m = s.shape[0] // 2
    e = jnp.exp2(s - jnp.max(s, axis=-1, keepdims=True))
    z = jnp.sum(e, axis=-1, keepdims=True)
    c = lam * z[:m] / z[m:]
    pd_ref[idx] = (e[:m] - c * e[m:]).astype(BF16)
    rz_ref[idx] = jnp.broadcast_to(1.0 / z[:m], (m, V_DIM))


def _attn_values(pd_ref, rz_ref, idx, vh, subg):
    o = _dot(pd_ref[idx], vh) * rz_ref[idx]
    return (_rms(o, subg) * (1.0 - LAMBDA_INIT)).astype(BF16)


def _short_conv(bg, cg, xc, convw_ref, seq):
    g = cg * xc
    m = g.shape[0]
    row = lax.broadcasted_iota(jnp.int32, g.shape, 0) % seq
    g_prev = jnp.where(row == 0, 0.0, pltpu.roll(g, 1, 0))
    g_next = jnp.where(row == seq - 1, 0.0, pltpu.roll(g, m - 1, 0))
    conv = g_prev * convw_ref[0:1, :] + g * convw_ref[1:2, :] + g_next * convw_ref[2:3, :]
    return bg * conv


def _pre_mod(x, mod_ref, gpre_ref, sub, row):
    sh, sc, _ = _mod_rows(mod_ref, sub, row)
    return _rms(x, gpre_ref[sub:sub + 1, :]) * (1.0 + sc) + sh


def _mix_ctx_kernel(x_ref, mod_ref, gpre_ref, gpost_ref, win_ref, convw_ref, lam_ref, subg_ref, wo_ref,
                    o_ref, k_ref, v_ref, u_ref, q_ref, kb_ref, vb_ref, cat_ref, s_ref, pd_ref, rz_ref, cv_ref,
                    mixc_ref):
    A = ATTN_WIDTH
    x = x_ref[...]
    u_ref[...] = _pre_mod(x, mod_ref, gpre_ref, 1, 0).astype(BF16)
    q_ref[...] = _dot(u_ref[...], win_ref[:, 0:A].astype(BF16)) * QK_SCALE_LOG2
    k = _dot(u_ref[...], win_ref[:, A:2 * A].astype(BF16))
    v = _dot(u_ref[...], win_ref[:, 2 * A:3 * A].astype(BF16))
    kb_ref[...] = k.astype(BF16)
    vb_ref[...] = v.astype(BF16)
    for h in range(N_HEADS):
        k_ref[pl.ds(h, CTX_TM, stride=N_HEADS), :] = k[:, h * V_DIM:(h + 1) * V_DIM]
        v_ref[pl.ds(h, CTX_TM, stride=N_HEADS), :] = v[:, h * V_DIM:(h + 1) * V_DIM]
    lam = _lambda(lam_ref)
    subg = subg_ref[...]
    n_units = (CTX_TM // SEQ) * N_HEADS
    n_conv_pieces = 3 * CONV_WIDTH // MXU_TILE

    def unit_slices(uid):
        b, h = divmod(uid, N_HEADS)
        return slice(b * SEQ, (b + 1) * SEQ), slice(h * V_DIM, (h + 1) * V_DIM)

    def conv_piece(i):
        c0 = 3 * A + i * MXU_TILE
        cv_ref[:, i * MXU_TILE:(i + 1) * MXU_TILE] = _dot(u_ref[...], win_ref[:, c0:c0 + MXU_TILE].astype(BF16))

    def mix_conv_piece(j):
        cols = slice(j * MXU_TILE, (j + 1) * MXU_TILE)
        mixc_ref[:, cols] = _dot(cat_ref[:, A:], wo_ref[A:, cols].astype(BF16))

    for uid in range(n_units):
        rows, cols = unit_slices(uid)
        q0, q1 = _split_maps(q_ref[rows, cols])
        _attn_scores(s_ref, uid, jnp.concatenate([q0, q1], axis=0), kb_ref[rows, cols])
    for uid in range(n_units):
        _attn_probs(s_ref, pd_ref, rz_ref, uid, lam)
        if uid < n_conv_pieces:
            conv_piece(uid)
    bg = cv_ref[:, 0:CONV_WIDTH]
    cg = cv_ref[:, CONV_WIDTH:2 * CONV_WIDTH]
    xc = cv_ref[:, 2 * CONV_WIDTH:]
    cat_ref[:, A:] = _short_conv(bg, cg, xc, convw_ref, SEQ).astype(BF16)
    for uid in range(n_units):
        rows, cols = unit_slices(uid)
        cat_ref[rows, cols] = _attn_values(pd_ref, rz_ref, uid, vb_ref[rows, cols], subg)
        if uid % 2 == 1:
            mix_conv_piece(uid // 2)
    mix = mixc_ref[...] + _dot(cat_ref[:, :A], wo_ref[:A, :].astype(BF16))
    gt = _mod_rows(mod_ref, 1, 0)[2]
    o_ref[...] = x + gt * _rms(mix, gpost_ref[1:2, :])


def _mix_ctx(x2d, mod3, gpre, gpost, win, convw, lamqk, subg, wo):
    n = x2d.shape[0]
    tm = CTX_TM
    n_units = (tm // SEQ) * N_HEADS
    assert 3 * CONV_WIDTH // MXU_TILE <= n_units and D_MODEL // MXU_TILE == n_units // 2
    row = pl.BlockSpec((tm, D_MODEL), lambda t: (t, 0))
    kv = pl.BlockSpec((tm * N_HEADS, V_DIM), lambda t: (t, 0))
    return pl.pallas_call(
        _mix_ctx_kernel,
        grid=(n // tm,),
        in_specs=[
            row,
            _mod_spec(),
            _const_spec((N_SUB, D_MODEL)),
            _const_spec((N_SUB, D_MODEL)),
            _const_spec((D_MODEL, IN_WIDTH)),
            _const_spec((3, CONV_WIDTH)),
            _const_spec((4, HEAD_DIM)),
            _const_spec((1, V_DIM)),
            _const_spec((D_MODEL, D_MODEL)),
        ],
        out_specs=[row, kv, kv],
        out_shape=[
            jax.ShapeDtypeStruct((n, D_MODEL), F32),
            jax.ShapeDtypeStruct((n * N_HEADS, V_DIM), F32),
            jax.ShapeDtypeStruct((n * N_HEADS, V_DIM), F32),
        ],
        scratch_shapes=[
            pltpu.VMEM((tm, D_MODEL), BF16),
            pltpu.VMEM((tm, ATTN_WIDTH), F32),
            pltpu.VMEM((tm, ATTN_WIDTH), BF16),
            pltpu.VMEM((tm, ATTN_WIDTH), BF16),
            pltpu.VMEM((tm, D_MODEL), BF16),
            pltpu.VMEM((n_units, 2 * SEQ, SEQ), F32),
            pltpu.VMEM((n_units, SEQ, SEQ), BF16),
            pltpu.VMEM((n_units, SEQ, V_DIM), F32),
            pltpu.VMEM((tm, 3 * CONV_WIDTH), F32),
            pltpu.VMEM((tm, D_MODEL), F32),
        ],
        compiler_params=pltpu.CompilerParams(vmem_limit_bytes=VMEM_LIMIT),
        name="mixer_ctx",
    )(x2d, mod3, gpre, gpost, win, convw, lamqk, subg, wo)


def _rope(xh, cos, sin_lo, sin_hi):
    return xh * cos + pltpu.roll(xh, 16, 1) * sin_hi + pltpu.roll(xh, V_DIM - 16, 1) * sin_lo


def _rope_tables():
    t = np.arange(DEC_SEQ)
    row = (t // GRID_W).astype(np.float64)
    col = (t % GRID_W).astype(np.float64)
    half = HEAD_DIM // 2
    freqs = ROPE_BASE ** (-np.arange(0, half, 2, dtype=np.float64) / half)
    ang = np.concatenate([row[:, None] * freqs, row[:, None] * freqs,
                          col[:, None] * freqs, col[:, None] * freqs], axis=-1)
    cos, sin = np.cos(ang), np.sin(ang)
    first = ((np.arange(HEAD_DIM) % half) < half // 2)[None, :]
    sin_lo = np.where(first, -sin, 0.0)
    sin_hi = np.where(first, 0.0, sin)
    tile = lambda a: jnp.asarray(np.concatenate([a, a], axis=-1), dtype=F32)
    return tile(cos), tile(sin_lo), tile(sin_hi)


def _lat_proj_kernel(x_ref, mod_ref, gpre_ref, win_ref, convw_ref, ck_ref, cv_ref, cos_ref, slo_ref, shi_ref,
                     qs_ref, kall_ref, vall_ref, conv_ref, u_ref):
    A, T, L = ATTN_WIDTH, DEC_SEQ, PAST_LEN
    u_ref[...] = _pre_mod(x_ref[...], mod_ref, gpre_ref, 1, 1 + pl.program_id(0)).astype(BF16)
    cos, slo, shi = cos_ref[...], slo_ref[...], shi_ref[...]
    q = _dot(u_ref[...], win_ref[:, 0:A].astype(BF16))
    for h in range(N_HEADS):
        cols = slice(h * V_DIM, (h + 1) * V_DIM)
        q0, q1 = _split_maps(_rope(q[:, cols], cos, slo, shi) * QK_SCALE_LOG2)
        for qb in range(T // LAT_QB):
            rows = slice(qb * LAT_QB, (qb + 1) * LAT_QB)
            qs_ref[0, h, qb, 0:LAT_QB, :] = q0[rows]
            qs_ref[0, h, qb, LAT_QB:, :] = q1[rows]
    k = _dot(u_ref[...], win_ref[:, A:2 * A].astype(BF16))
    for h in range(N_HEADS):
        cols = slice(h * V_DIM, (h + 1) * V_DIM)
        kall_ref[0, h, 0:L, :] = ck_ref[0, pl.ds(h, L, stride=N_HEADS), :].astype(BF16)
        kall_ref[0, h, L:, :] = _rope(k[:, cols], cos, slo, shi).astype(BF16)
    v = _dot(u_ref[...], win_ref[:, 2 * A:3 * A].astype(BF16))
    for h in range(N_HEADS):
        cols = slice(h * V_DIM, (h + 1) * V_DIM)
        vall_ref[0, h, 0:L, :] = cv_ref[0, pl.ds(h, L, stride=N_HEADS), :].astype(BF16)
        vall_ref[0, h, L:, :] = v[:, cols].astype(BF16)
    ub = u_ref[...]
    bg = _dot(ub, win_ref[:, 3 * A:3 * A + CONV_WIDTH].astype(BF16))
    cg = _dot(ub, win_ref[:, 3 * A + CONV_WIDTH:3 * A + 2 * CONV_WIDTH].astype(BF16))
    xc = _dot(ub, win_ref[:, 3 * A + 2 * CONV_WIDTH:].astype(BF16))
    conv_ref[...] = _short_conv(bg, cg, xc, convw_ref, T).astype(BF16)


def _lat_proj(x2d, mod3, gpre, win, convw, ck, cv):
    n = x2d.shape[0]
    T, L = DEC_SEQ, PAST_LEN
    nb = n // T
    nqb = T // LAT_QB
    cos, slo, shi = _rope_tables()
    cache = pl.BlockSpec((1, L * N_HEADS, V_DIM), lambda b: (b, 0, 0))
    return pl.pallas_call(
        _lat_proj_kernel,
        grid=(nb,),
        in_specs=[
            pl.BlockSpec((T, D_MODEL), lambda b: (b, 0)),
            _mod_spec(),
            _const_spec((N_SUB, D_MODEL)),
            _const_spec((D_MODEL, IN_WIDTH)),
            _const_spec((3, CONV_WIDTH)),
            cache, cache,
            _const_spec((T, V_DIM)), _const_spec((T, V_DIM)), _const_spec((T, V_DIM)),
        ],
        out_specs=[
            pl.BlockSpec((1, N_HEADS, nqb, 2 * LAT_QB, V_DIM), lambda b: (b, 0, 0, 0, 0)),
            pl.BlockSpec((1, N_HEADS, L + T, V_DIM), lambda b: (b, 0, 0, 0)),
            pl.BlockSpec((1, N_HEADS, L + T, V_DIM), lambda b: (b, 0, 0, 0)),
            pl.BlockSpec((T, CONV_WIDTH), lambda b: (b, 0)),
        ],
        out_shape=[
            jax.ShapeDtypeStruct((nb, N_HEADS, nqb, 2 * LAT_QB, V_DIM), BF16),
            jax.ShapeDtypeStruct((nb, N_HEADS, L + T, V_DIM), BF16),
            jax.ShapeDtypeStruct((nb, N_HEADS, L + T, V_DIM), BF16),
            jax.ShapeDtypeStruct((n, CONV_WIDTH), BF16),
        ],
        scratch_shapes=[pltpu.VMEM((T, D_MODEL), BF16)],
        compiler_params=pltpu.CompilerParams(vmem_limit_bytes=VMEM_LIMIT),
        name="mixer_lat_proj",
    )(x2d, mod3, gpre, win, convw, ck, cv, cos, slo, shi)


def _lat_attn_kernel(x_ref, mod_ref, gpost_ref, qs_ref, kall_ref, vall_ref, conv_ref, lam_ref, subg_ref, wo_ref,
                     o_ref, cat_ref, s_ref, pd_ref, rz_ref):
    A = ATTN_WIDTH
    lam = _lambda(lam_ref)
    subg = subg_ref[...]

    def scores(h):
        _attn_scores(s_ref, h % 2, qs_ref[0, h, 0], kall_ref[0, h])

    def probs(h):
        _attn_probs(s_ref, pd_ref, rz_ref, h % 2, lam)

    def values(h):
        cat_ref[:, h * V_DIM:(h + 1) * V_DIM] = _attn_values(pd_ref, rz_ref, h % 2, vall_ref[0, h], subg)

    scores(0)
    scores(1)
    mix_conv = _dot(conv_ref[...], wo_ref[A:, :].astype(BF16))
    for h in range(N_HEADS):
        probs(h)
        if h + 2 < N_HEADS:
            scores(h + 2)
        if h >= 1:
            values(h - 1)
    values(N_HEADS - 1)
    mix = mix_conv + _dot(cat_ref[...], wo_ref[:A, :].astype(BF16))
    gt = _mod_rows(mod_ref, 1, 1 + pl.program_id(0))[2]
    o_ref[...] = x_ref[...] + gt * _rms(mix, gpost_ref[1:2, :])


def _lat_attn(x2d, mod3, gpost, qs, kall, vall, conv, lamqk, subg, wo):
    n = x2d.shape[0]
    T, L = DEC_SEQ, PAST_LEN
    nb = n // T
    nqb = T // LAT_QB
    row = pl.BlockSpec((LAT_QB, D_MODEL), lambda b, i: (b * nqb + i, 0))
    keys = pl.BlockSpec((1, N_HEADS, L + T, V_DIM), lambda b, i: (b, 0, 0, 0))
    return pl.pallas_call(
        _lat_attn_kernel,
        grid=(nb, nqb),
        in_specs=[
            row,
            _mod_spec(),
            _const_spec((N_SUB, D_MODEL)),
            pl.BlockSpec((1, N_HEADS, 1, 2 * LAT_QB, V_DIM), lambda b, i: (b, 0, i, 0, 0)),
            keys, keys,
            pl.BlockSpec((LAT_QB, CONV_WIDTH), lambda b, i: (b * nqb + i, 0)),
            _const_spec((4, HEAD_DIM)),
            _const_spec((1, V_DIM)),
            _const_spec((D_MODEL, D_MODEL)),
        ],
        out_specs=row,
        out_shape=jax.ShapeDtypeStruct((n, D_MODEL), F32),
        scratch_shapes=[
            pltpu.VMEM((LAT_QB, ATTN_WIDTH), BF16),
            pltpu.VMEM((2, 2 * LAT_QB, L + T), F32),
            pltpu.VMEM((2, LAT_QB, L + T), BF16),
            pltpu.VMEM((2, LAT_QB, V_DIM), F32),
        ],
        compiler_params=pltpu.CompilerParams(vmem_limit_bytes=VMEM_LIMIT),
        name="mixer_lat_attn",
    )(x2d, mod3, gpost, qs, kall, vall, conv, lamqk, subg, wo)


def kernel(x_prompt, x_sample, c, cache_k, cache_v, c_ctx, w_mod, b_mod, norm_pre, norm_post,
           ffn1_up, ffn1_down, ffn2_up, ffn2_down, w_in, conv_w, lam_qk, subln_g, w_o):
    batch, seq, _ = x_prompt.shape
    dec_batch, dec_seq, _ = x_sample.shape
    assert (seq, dec_seq) == (SEQ, DEC_SEQ) and cache_k.shape[1] == 1

    mod = _modulation(c_ctx.reshape(1, D_MODEL), c, w_mod[0], b_mod)

    gpre, gpost = norm_pre[0], norm_post[0]
    up1, dn1, up2, dn2 = ffn1_up[0], ffn1_down[0], ffn2_up[0], ffn2_down[0]
    win, wo = w_in[0], w_o[0]
    convw, lamqk, subg = conv_w[0], lam_qk[0], subln_g

    xp = x_prompt.reshape(batch * seq, D_MODEL)
    xs = x_sample.reshape(dec_batch * dec_seq, D_MODEL)
    ck = cache_k.reshape(dec_batch, PAST_LEN * N_HEADS, 2 * HEAD_DIM)
    cv = cache_v.reshape(dec_batch, PAST_LEN * N_HEADS, V_DIM)

    xp, xs = _ffn(xp, xs, mod, gpre, gpost, up1, dn1, sub=0)
    xp, new_k, new_v = _mix_ctx(xp, mod, gpre, gpost, win, convw, lamqk, subg, wo)
    qs, kall, vall, conv = _lat_proj(xs, mod, gpre, win, convw, ck, cv)
    xs = _lat_attn(xs, mod, gpost, qs, kall, vall, conv, lamqk, subg, wo)
    xp, xs = _ffn(xp, xs, mod, gpre, gpost, up2, dn2, sub=2)

    return (xp.reshape(batch, seq, D_MODEL), xs.reshape(dec_batch, dec_seq, D_MODEL),
            new_k.reshape(batch, 1, seq, N_HEADS, 2 * HEAD_DIM), new_v.reshape(batch, 1, seq, N_HEADS, V_DIM))
```

```python
import functools
import math

import jax
import jax.numpy as jnp
import numpy as np
from jax import lax
from jax.experimental import pallas as pl
from jax.experimental.pallas import tpu as pltpu

D_MODEL = 1024
SEQ = 256
DEC_SEQ = 1024
PAST_LEN = 512
GRID_W = 64
ATTN_WIDTH = 512
CONV_WIDTH = 512
N_HEADS = 4
HEAD_DIM = 64
V_DIM = 128
D_FF = 2816
ROPE_BASE = 10000.0
EPS = 1e-6
N_SUB = 3
IN_WIDTH = 3 * ATTN_WIDTH + 3 * CONV_WIDTH
LAMBDA_INIT = 0.8 - 0.6 * math.exp(-0.3 * 0)
QK_SCALE_LOG2 = HEAD_DIM ** -0.5 * math.log2(math.e)

F32 = jnp.float32
BF16 = jnp.bfloat16

MXU_TILE = 256
COND_ROWS = 8
FFN_STEP_ROWS = 1024
FFN_SUB = 512
FFN_CHUNKS = ((0, 1024), (1024, 2048), (2048, D_FF))
W_PIECE = MXU_TILE
N_PIECES = D_FF // W_PIECE
CTX_TM = 512
LAT_QB = 512
VMEM_LIMIT = 60 * 1024 * 1024


def _rms_normalise(x):
    return x * lax.rsqrt(jnp.mean(x * x, axis=-1, keepdims=True) + EPS)


def _dot(a, b):
    return jnp.dot(a, b, preferred_element_type=F32)


def _dot_nt(a, b):
    return lax.dot_general(a, b, (((1,), (1,)), ((), ())), preferred_element_type=F32)


def _const_spec(shape):
    nd = len(shape)
    return pl.BlockSpec(shape, lambda *_: (0,) * nd, pipeline_mode=pl.Buffered(1))


def _mod_kernel(cctx_ref, c_ref, w_ref, b_ref, o_ref, cond_ref):
    nlat = c_ref.shape[0]
    cond_ref[...] = jnp.zeros_like(cond_ref)
    cond_ref[0:1, :] = cctx_ref[...]
    cond_ref[1:1 + nlat, :] = c_ref[...]
    c = cond_ref[...]
    s = (c * jax.nn.sigmoid(c)).astype(BF16)
    o_ref[0] = _dot(s, w_ref[...].astype(BF16)) + b_ref[...]


def _modulation(c_ctx, c, w_mod, b_mod):
    n_out = w_mod.shape[1]
    assert n_out == 3 * N_SUB * D_MODEL and 1 + c.shape[0] <= COND_ROWS
    return pl.pallas_call(
        _mod_kernel,
        grid=(3 * N_SUB,),
        in_specs=[
            pl.BlockSpec((1, D_MODEL), lambda j: (0, 0)),
            pl.BlockSpec(c.shape, lambda j: (0, 0)),
            pl.BlockSpec((D_MODEL, D_MODEL), lambda j: (0, j)),
            pl.BlockSpec((1, D_MODEL), lambda j: (0, j)),
        ],
        out_specs=pl.BlockSpec((1, COND_ROWS, D_MODEL), lambda j: (j, 0, 0)),
        out_shape=jax.ShapeDtypeStruct((3 * N_SUB, COND_ROWS, D_MODEL), F32),
        scratch_shapes=[pltpu.VMEM((COND_ROWS, D_MODEL), F32)],
        name="modulation",
    )(c_ctx, c, w_mod, b_mod)


def _mod_spec():
    return _const_spec((3 * N_SUB, COND_ROWS, D_MODEL))


def _mod_rows(mod_ref, sub, row):
    return tuple(mod_ref[3 * sub + i, pl.ds(row, 1), :] for i in range(3))


def _ffn_kernel(mod_ref, gpre_ref, gpost_ref, xp_hbm, xs_hbm, wup_hbm, wdn_hbm, op_hbm, os_hbm,
                wup_ref, wdn_ref, sa_ref, sb_ref, sd_ref, wsem, xbuf, xsem, obuf, osem, u_ref, acc_ref,
                *, sub, n_ctx, n_lat):
    t = pl.program_id(0)
    n = n_ctx + n_lat
    slot = t % 2
    R, S = FFN_STEP_ROWS, FFN_SUB
    n_sub = R // S

    def tile_dma(k, k_slot, action, into_vmem):
        def go(hbm_ref, blk):
            rows = hbm_ref.at[pl.ds(blk * R, R), :]
            if into_vmem:
                cp = pltpu.make_async_copy(rows, xbuf.at[k_slot], xsem.at[k_slot])
            else:
                cp = pltpu.make_async_copy(obuf.at[k_slot], rows, osem.at[k_slot])
            getattr(cp, action)()
        ctx_ref, lat_ref = (xp_hbm, xs_hbm) if into_vmem else (op_hbm, os_hbm)
        pl.when(k < n_ctx)(lambda: go(ctx_ref, k))
        pl.when(k >= n_ctx)(lambda: go(lat_ref, k - n_ctx))

    def weight_piece(j):
        ps = j % 2
        c = pl.ds(pl.multiple_of(j * W_PIECE, W_PIECE), W_PIECE)
        c2 = pl.ds(pl.multiple_of(D_FF + j * W_PIECE, W_PIECE), W_PIECE)
        return (
            pltpu.make_async_copy(wup_hbm.at[:, c], sa_ref.at[ps], wsem.at[ps, 0]),
            pltpu.make_async_copy(wup_hbm.at[:, c2], sb_ref.at[ps], wsem.at[ps, 1]),
            pltpu.make_async_copy(wdn_hbm.at[c, :], sd_ref.at[ps], wsem.at[ps, 2]),
        )

    def start_piece(j):
        for cp in weight_piece(j):
            cp.start()

    def convert_piece(j, carry):
        for cp in weight_piece(j):
            cp.wait()
        ps = j % 2
        wup_ref[0, j] = sa_ref[ps].astype(BF16)
        wup_ref[1, j] = sb_ref[ps].astype(BF16)
        wdn_ref[j] = sd_ref[ps].astype(BF16)
        pl.when(j + 2 < N_PIECES)(lambda: start_piece(j + 2))
        return carry

    sh, sc, gt = _mod_rows(mod_ref, sub, jnp.where(t < n_ctx, 0, 1 + t - n_ctx))
    pre_gain = gpre_ref[sub:sub + 1, :] * (1.0 + sc)
    post_gain = gpost_ref[sub:sub + 1, :] * (0.5 * gt)

    def pre(s):
        x = xbuf[slot, s * S:(s + 1) * S, :]
        u_ref[s] = (_rms_normalise(x) * pre_gain + sh).astype(BF16)

    def mm(s, ci):
        lo, hi = FFN_CHUNKS[ci]
        pieces = range(lo // W_PIECE, hi // W_PIECE)
        ub = u_ref[s]
        a = jnp.concatenate([_dot(ub, wup_ref[0, j]) for j in pieces], axis=1)
        b = jnp.concatenate([_dot(ub, wup_ref[1, j]) for j in pieces], axis=1)
        h = (a * jax.nn.sigmoid(a) * b).astype(BF16)
        part = _dot(h, wdn_ref[pieces.start:pieces.stop].reshape(hi - lo, D_MODEL))
        if ci == 0:
            acc_ref[s] = part
        else:
            acc_ref[s] += part

    def post(s):
        x = xbuf[slot, s * S:(s + 1) * S, :]
        obuf[slot, s * S:(s + 1) * S, :] = x + _rms_normalise(acc_ref[s]) * post_gain

    @pl.when(t == 0)
    def _():
        tile_dma(t, slot, "start", True)
        start_piece(0)
        start_piece(1)

    tile_dma(t, slot, "wait", True)
    pl.when(t + 1 < n)(lambda: tile_dma(t + 1, 1 - slot, "start", True))
    pl.when(t >= 2)(lambda: tile_dma(t - 2, slot, "wait", False))

    @pl.when(t == 0)
    def _():
        lax.fori_loop(0, N_PIECES, convert_piece, 0)

    for s in range(n_sub):
        pre(s)
    for ci in range(len(FFN_CHUNKS)):
        for s in range(n_sub):
            mm(s, ci)
    for s in range(n_sub):
        post(s)

    tile_dma(t, slot, "start", False)

    @pl.when(t == n - 1)
    def _():
        if n >= 2:
            tile_dma(t - 1, 1 - slot, "wait", False)
        tile_dma(t, slot, "wait", False)


def _ffn(xp2d, xs2d, mod3, gpre, gpost, wup, wdn, *, sub):
    R = FFN_STEP_ROWS
    n_ctx, n_lat = xp2d.shape[0] // R, xs2d.shape[0] // R
    assert R == DEC_SEQ and xp2d.shape[0] % R == 0 and xs2d.shape[0] % R == 0
    any_spec = pl.BlockSpec(memory_space=pl.ANY)
    return pl.pallas_call(
        functools.partial(_ffn_kernel, sub=sub, n_ctx=n_ctx, n_lat=n_lat),
        grid=(n_ctx + n_lat,),
        in_specs=[
            _mod_spec(),
            _const_spec((N_SUB, D_MODEL)),
            _const_spec((N_SUB, D_MODEL)),
            any_spec, any_spec, any_spec, any_spec,
        ],
        out_specs=[any_spec, any_spec],
        out_shape=[jax.ShapeDtypeStruct(xp2d.shape, F32), jax.ShapeDtypeStruct(xs2d.shape, F32)],
        scratch_shapes=[
            pltpu.VMEM((2, N_PIECES, D_MODEL, W_PIECE), BF16),
            pltpu.VMEM((N_PIECES, W_PIECE, D_MODEL), BF16),
            pltpu.VMEM((2, D_MODEL, W_PIECE), F32),
            pltpu.VMEM((2, D_MODEL, W_PIECE), F32),
            pltpu.VMEM((2, W_PIECE, D_MODEL), F32),
            pltpu.SemaphoreType.DMA((2, 3)),
            pltpu.VMEM((2, R, D_MODEL), F32),
            pltpu.SemaphoreType.DMA((2,)),
            pltpu.VMEM((2, R, D_MODEL), F32),
            pltpu.SemaphoreType.DMA((2,)),
            pltpu.VMEM((R // FFN_SUB, FFN_SUB, D_MODEL), BF16),
            pltpu.VMEM((R // FFN_SUB, FFN_SUB, D_MODEL), F32),
        ],
        compiler_params=pltpu.CompilerParams(dimension_semantics=("arbitrary",), vmem_limit_bytes=VMEM_LIMIT),
        name=f"ffn{sub}",
    )(mod3, gpre, gpost, xp2d, xs2d, wup, wdn)


def _lambda(lam_ref):
    lq = lam_ref[...]
    l1 = jnp.sum(lq[0:1, :] * lq[1:2, :], axis=-1, keepdims=True)
    l2 = jnp.sum(lq[2:3, :] * lq[3:4, :], axis=-1, keepdims=True)
    return jnp.exp(l1) - jnp.exp(l2) + LAMBDA_INIT


def _split_maps(qh):
    lane = lax.broadcasted_iota(jnp.int32, qh.shape, 1)
    q0 = jnp.where(lane < HEAD_DIM, qh, 0.0).astype(BF16)
    q1 = jnp.where(lane >= HEAD_DIM, qh, 0.0).astype(BF16)
    return q0, q1


def _attn_scores(s_ref, idx, qs, kh):
    s_ref[idx] = _dot_nt(qs, kh)


def _attn_probs(s_ref, pd_ref, rz_ref, idx, lam):
    s = s_ref[idx]
    m = s.shape[0] // 2
    e = jnp.exp2(s - jnp.max(s, axis=-1, keepdims=True))
    z = jnp.sum(e, axis=-1, keepdims=True)
    c = lam * z[:m] / z[m:]
    pd_ref[idx] = (e[:m] - c * e[m:]).astype(BF16)
    rz_ref[idx] = jnp.broadcast_to(1.0 / z[:m], (m, V_DIM))


def _attn_values(pd_ref, rz_ref, idx, vh, head_gain):
    o = _dot(pd_ref[idx], vh) * rz_ref[idx]
    return (_rms_normalise(o) * head_gain).astype(BF16)


def _short_conv(bg, cg, xc, convw_ref, seq):
    g = cg * xc
    m = g.shape[0]
    row = lax.broadcasted_iota(jnp.int32, g.shape, 0) % seq
    g_prev = jnp.where(row == 0, 0.0, pltpu.roll(g, 1, 0))
    g_next = jnp.where(row == seq - 1, 0.0, pltpu.roll(g, m - 1, 0))
    conv = g_prev * convw_ref[0:1, :] + g * convw_ref[1:2, :] + g_next * convw_ref[2:3, :]
    return bg * conv


def _pre_mod(x, mod_ref, gpre_ref, sub, row):
    sh, sc, _ = _mod_rows(mod_ref, sub, row)
    return _rms_normalise(x) * (gpre_ref[sub:sub + 1, :] * (1.0 + sc)) + sh


def _mix_ctx_kernel(x_ref, mod_ref, gpre_ref, gpost_ref, win_ref, convw_ref, lam_ref, subg_ref, wo_ref,
                    o_ref, k_ref, v_ref, u_ref, q_ref, kb_ref, vb_ref, cat_ref, s_ref, pd_ref, rz_ref, cv_ref,
                    mixc_ref):
    A = ATTN_WIDTH
    x = x_ref[...]
    u_ref[...] = _pre_mod(x, mod_ref, gpre_ref, 1, 0).astype(BF16)
    q_ref[...] = _dot(u_ref[...], win_ref[:, 0:A].astype(BF16)) * QK_SCALE_LOG2
    k = _dot(u_ref[...], win_ref[:, A:2 * A].astype(BF16))
    v = _dot(u_ref[...], win_ref[:, 2 * A:3 * A].astype(BF16))
    kb_ref[...] = k.astype(BF16)
    vb_ref[...] = v.astype(BF16)
    for h in range(N_HEADS):
        k_ref[pl.ds(h, CTX_TM, stride=N_HEADS), :] = k[:, h * V_DIM:(h + 1) * V_DIM]
        v_ref[pl.ds(h, CTX_TM, stride=N_HEADS), :] = v[:, h * V_DIM:(h + 1) * V_DIM]
    lam = _lambda(lam_ref)
    head_gain = subg_ref[...] * (1.0 - LAMBDA_INIT)
    n_units = (CTX_TM // SEQ) * N_HEADS
    n_conv_pieces = 3 * CONV_WIDTH // MXU_TILE

    def unit_slices(uid):
        b, h = divmod(uid, N_HEADS)
        return slice(b * SEQ, (b + 1) * SEQ), slice(h * V_DIM, (h + 1) * V_DIM)

    def conv_piece(i):
        c0 = 3 * A + i * MXU_TILE
        cv_ref[:, i * MXU_TILE:(i + 1) * MXU_TILE] = _dot(u_ref[...], win_ref[:, c0:c0 + MXU_TILE].astype(BF16))

    def mix_conv_piece(j):
        cols = slice(j * MXU_TILE, (j + 1) * MXU_TILE)
        mixc_ref[:, cols] = _dot(cat_ref[:, A:], wo_ref[A:, cols].astype(BF16))

    for uid in range(n_units):
        rows, cols = unit_slices(uid)
        q0, q1 = _split_maps(q_ref[rows, cols])
        _attn_scores(s_ref, uid, jnp.concatenate([q0, q1], axis=0), kb_ref[rows, cols])
    for uid in range(n_units):
        _attn_probs(s_ref, pd_ref, rz_ref, uid, lam)
        if uid < n_conv_pieces:
            conv_piece(uid)
    bg = cv_ref[:, 0:CONV_WIDTH]
    cg = cv_ref[:, CONV_WIDTH:2 * CONV_WIDTH]
    xc = cv_ref[:, 2 * CONV_WIDTH:]
    cat_ref[:, A:] = _short_conv(bg, cg, xc, convw_ref, SEQ).astype(BF16)
    for uid in range(n_units):
        rows, cols = unit_slices(uid)
        cat_ref[rows, cols] = _attn_values(pd_ref, rz_ref, uid, vb_ref[rows, cols], head_gain)
        if uid % 2 == 1:
            mix_conv_piece(uid // 2)
    mix = mixc_ref[...] + _dot(cat_ref[:, :A], wo_ref[:A, :].astype(BF16))
    gt = _mod_rows(mod_ref, 1, 0)[2]
    o_ref[...] = x + _rms_normalise(mix) * (gpost_ref[1:2, :] * gt)


def _mix_ctx(x2d, mod3, gpre, gpost, win, convw, lamqk, subg, wo):
    n = x2d.shape[0]
    tm = CTX_TM
    n_units = (tm // SEQ) * N_HEADS
    assert 3 * CONV_WIDTH // MXU_TILE <= n_units and D_MODEL // MXU_TILE == n_units // 2
    row = pl.BlockSpec((tm, D_MODEL), lambda t: (t, 0))
    kv = pl.BlockSpec((tm * N_HEADS, V_DIM), lambda t: (t, 0))
    return pl.pallas_call(
        _mix_ctx_kernel,
        grid=(n // tm,),
        in_specs=[
            row,
            _mod_spec(),
            _const_spec((N_SUB, D_MODEL)),
            _const_spec((N_SUB, D_MODEL)),
            _const_spec((D_MODEL, IN_WIDTH)),
            _const_spec((3, CONV_WIDTH)),
            _const_spec((4, HEAD_DIM)),
            _const_spec((1, V_DIM)),
            _const_spec((D_MODEL, D_MODEL)),
        ],
        out_specs=[row, kv, kv],
        out_shape=[
            jax.ShapeDtypeStruct((n, D_MODEL), F32),
            jax.ShapeDtypeStruct((n * N_HEADS, V_DIM), F32),
            jax.ShapeDtypeStruct((n * N_HEADS, V_DIM), F32),
        ],
        scratch_shapes=[
            pltpu.VMEM((tm, D_MODEL), BF16),
            pltpu.VMEM((tm, ATTN_WIDTH), F32),
            pltpu.VMEM((tm, ATTN_WIDTH), BF16),
            pltpu.VMEM((tm, ATTN_WIDTH), BF16),
            pltpu.VMEM((tm, D_MODEL), BF16),
            pltpu.VMEM((n_units, 2 * SEQ, SEQ), F32),
            pltpu.VMEM((n_units, SEQ, SEQ), BF16),
            pltpu.VMEM((n_units, SEQ, V_DIM), F32),
            pltpu.VMEM((tm, 3 * CONV_WIDTH), F32),
            pltpu.VMEM((tm, D_MODEL), F32),
        ],
        compiler_params=pltpu.CompilerParams(vmem_limit_bytes=VMEM_LIMIT),
        name="mixer_ctx",
    )(x2d, mod3, gpre, gpost, win, convw, lamqk, subg, wo)


def _rope(xh, cos, sin_lo, sin_hi):
    return xh * cos + pltpu.roll(xh, 16, 1) * sin_hi + pltpu.roll(xh, V_DIM - 16, 1) * sin_lo


def _rope_tables():
    t = np.arange(DEC_SEQ)
    row = (t // GRID_W).astype(np.float64)
    col = (t % GRID_W).astype(np.float64)
    half = HEAD_DIM // 2
    freqs = ROPE_BASE ** (-np.arange(0, half, 2, dtype=np.float64) / half)
    ang = np.concatenate([row[:, None] * freqs, row[:, None] * freqs,
                          col[:, None] * freqs, col[:, None] * freqs], axis=-1)
    cos, sin = np.cos(ang), np.sin(ang)
    first = ((np.arange(HEAD_DIM) % half) < half // 2)[None, :]
    sin_lo = np.where(first, -sin, 0.0)
    sin_hi = np.where(first, 0.0, sin)
    tile = lambda a: jnp.asarray(np.concatenate([a, a], axis=-1), dtype=F32)
    return tile(cos), tile(sin_lo), tile(sin_hi)


def _lat_proj_kernel(x_ref, mod_ref, gpre_ref, win_ref, convw_ref, ck_ref, cv_ref, cos_ref, slo_ref, shi_ref,
                     qs_ref, kall_ref, vall_ref, conv_ref, u_ref):
    A, T, L = ATTN_WIDTH, DEC_SEQ, PAST_LEN
    u_ref[...] = _pre_mod(x_ref[...], mod_ref, gpre_ref, 1, 1 + pl.program_id(0)).astype(BF16)
    cos, slo, shi = cos_ref[...], slo_ref[...], shi_ref[...]
    q = _dot(u_ref[...], win_ref[:, 0:A].astype(BF16))
    for h in range(N_HEADS):
        cols = slice(h * V_DIM, (h + 1) * V_DIM)
        q0, q1 = _split_maps(_rope(q[:, cols], cos, slo, shi) * QK_SCALE_LOG2)
        for qb in range(T // LAT_QB):
            rows = slice(qb * LAT_QB, (qb + 1) * LAT_QB)
            qs_ref[0, h, qb, 0:LAT_QB, :] = q0[rows]
            qs_ref[0, h, qb, LAT_QB:, :] = q1[rows]
    k = _dot(u_ref[...], win_ref[:, A:2 * A].astype(BF16))
    for h in range(N_HEADS):
        cols = slice(h * V_DIM, (h + 1) * V_DIM)
        kall_ref[0, h, 0:L, :] = ck_ref[0, pl.ds(h, L, stride=N_HEADS), :].astype(BF16)
        kall_ref[0, h, L:, :] = _rope(k[:, cols], cos, slo, shi).astype(BF16)
    v = _dot(u_ref[...], win_ref[:, 2 * A:3 * A].astype(BF16))
    for h in range(N_HEADS):
        cols = slice(h * V_DIM, (h + 1) * V_DIM)
        vall_ref[0, h, 0:L, :] = cv_ref[0, pl.ds(h, L, stride=N_HEADS), :].astype(BF16)
        vall_ref[0, h, L:, :] = v[:, cols].astype(BF16)
    ub = u_ref[...]
    bg = _dot(ub, win_ref[:, 3 * A:3 * A + CONV_WIDTH].astype(BF16))
    cg = _dot(ub, win_ref[:, 3 * A + CONV_WIDTH:3 * A + 2 * CONV_WIDTH].astype(BF16))
    xc = _dot(ub, win_ref[:, 3 * A + 2 * CONV_WIDTH:].astype(BF16))
    conv_ref[...] = _short_conv(bg, cg, xc, convw_ref, T).astype(BF16)


def _lat_proj(x2d, mod3, gpre, win, convw, ck, cv):
    n = x2d.shape[0]
    T, L = DEC_SEQ, PAST_LEN
    nb = n // T
    nqb = T // LAT_QB
    cos, slo, shi = _rope_tables()
    cache = pl.BlockSpec((1, L * N_HEADS, V_DIM), lambda b: (b, 0, 0))
    return pl.pallas_call(
        _lat_proj_kernel,
        grid=(nb,),
        in_specs=[
            pl.BlockSpec((T, D_MODEL), lambda b: (b, 0)),
            _mod_spec(),
            _const_spec((N_SUB, D_MODEL)),
            _const_spec((D_MODEL, IN_WIDTH)),
            _const_spec((3, CONV_WIDTH)),
            cache, cache,
            _const_spec((T, V_DIM)), _const_spec((T, V_DIM)), _const_spec((T, V_DIM)),
        ],
        out_specs=[
            pl.BlockSpec((1, N_HEADS, nqb, 2 * LAT_QB, V_DIM), lambda b: (b, 0, 0, 0, 0)),
            pl.BlockSpec((1, N_HEADS, L + T, V_DIM), lambda b: (b, 0, 0, 0)),
            pl.BlockSpec((1, N_HEADS, L + T, V_DIM), lambda b: (b, 0, 0, 0)),
            pl.BlockSpec((T, CONV_WIDTH), lambda b: (b, 0)),
        ],
        out_shape=[
            jax.ShapeDtypeStruct((nb, N_HEADS, nqb, 2 * LAT_QB, V_DIM), BF16),
            jax.ShapeDtypeStruct((nb, N_HEADS, L + T, V_DIM), BF16),
            jax.ShapeDtypeStruct((nb, N_HEADS, L + T, V_DIM), BF16),
            jax.ShapeDtypeStruct((n, CONV_WIDTH), BF16),
        ],
        scratch_shapes=[pltpu.VMEM((T, D_MODEL), BF16)],
        compiler_params=pltpu.CompilerParams(vmem_limit_bytes=VMEM_LIMIT),
        name="mixer_lat_proj",
    )(x2d, mod3, gpre, win, convw, ck, cv, cos, slo, shi)


def _lat_attn_kernel(x_ref, mod_ref, gpost_ref, qs_ref, kall_ref, vall_ref, conv_ref, lam_ref, subg_ref, wo_ref,
                     o_ref, cat_ref, s_ref, pd_ref, rz_ref):
    A = ATTN_WIDTH
    lam = _lambda(lam_ref)
    head_gain = subg_ref[...] * (1.0 - LAMBDA_INIT)

    def scores(h):
        _attn_scores(s_ref, h % 2, qs_ref[0, h, 0], kall_ref[0, h])

    def probs(h):
        _attn_probs(s_ref, pd_ref, rz_ref, h % 2, lam)

    def values(h):
        cat_ref[:, h * V_DIM:(h + 1) * V_DIM] = _attn_values(pd_ref, rz_ref, h % 2, vall_ref[0, h], head_gain)

    scores(0)
    scores(1)
    mix_conv = _dot(conv_ref[...], wo_ref[A:, :].astype(BF16))
    for h in range(N_HEADS):
        probs(h)
        if h + 2 < N_HEADS:
            scores(h + 2)
        if h >= 1:
            values(h - 1)
    values(N_HEADS - 1)
    mix = mix_conv + _dot(cat_ref[...], wo_ref[:A, :].astype(BF16))
    gt = _mod_rows(mod_ref, 1, 1 + pl.program_id(0))[2]
    o_ref[...] = x_ref[...] + _rms_normalise(mix) * (gpost_ref[1:2, :] * gt)


def _lat_attn(x2d, mod3, gpost, qs, kall, vall, conv, lamqk, subg, wo):
    n = x2d.shape[0]
    T, L = DEC_SEQ, PAST_LEN
    nb = n // T
    nqb = T // LAT_QB
    row = pl.BlockSpec((LAT_QB, D_MODEL), lambda b, i: (b * nqb + i, 0))
    keys = pl.BlockSpec((1, N_HEADS, L + T, V_DIM), lambda b, i: (b, 0, 0, 0))
    return pl.pallas_call(
        _lat_attn_kernel,
        grid=(nb, nqb),
        in_specs=[
            row,
            _mod_spec(),
            _const_spec((N_SUB, D_MODEL)),
            pl.BlockSpec((1, N_HEADS, 1, 2 * LAT_QB, V_DIM), lambda b, i: (b, 0, i, 0, 0)),
            keys, keys,
            pl.BlockSpec((LAT_QB, CONV_WIDTH), lambda b, i: (b * nqb + i, 0)),
            _const_spec((4, HEAD_DIM)),
            _const_spec((1, V_DIM)),
            _const_spec((D_MODEL, D_MODEL)),
        ],
        out_specs=row,
        out_shape=jax.ShapeDtypeStruct((n, D_MODEL), F32),
        scratch_shapes=[
            pltpu.VMEM((LAT_QB, ATTN_WIDTH), BF16),
            pltpu.VMEM((2, 2 * LAT_QB, L + T), F32),
            pltpu.VMEM((2, LAT_QB, L + T), BF16),
            pltpu.VMEM((2, LAT_QB, V_DIM), F32),
        ],
        compiler_params=pltpu.CompilerParams(vmem_limit_bytes=VMEM_LIMIT),
        name="mixer_lat_attn",
    )(x2d, mod3, gpost, qs, kall, vall, conv, lamqk, subg, wo)


def kernel(x_prompt, x_sample, c, cache_k, cache_v, c_ctx, w_mod, b_mod, norm_pre, norm_post,
           ffn1_up, ffn1_down, ffn2_up, ffn2_down, w_in, conv_w, lam_qk, subln_g, w_o):
    batch, seq, _ = x_prompt.shape
    dec_batch, dec_seq, _ = x_sample.shape
    assert (seq, dec_seq) == (SEQ, DEC_SEQ) and cache_k.shape[1] == 1

    mod = _modulation(c_ctx.reshape(1, D_MODEL), c, w_mod[0], b_mod)

    gpre, gpost = norm_pre[0], norm_post[0]
    up1, dn1, up2, dn2 = ffn1_up[0], ffn1_down[0], ffn2_up[0], ffn2_down[0]
    win, wo = w_in[0], w_o[0]
    convw, lamqk, subg = conv_w[0], lam_qk[0], subln_g

    xp = x_prompt.reshape(batch * seq, D_MODEL)
    xs = x_sample.reshape(dec_batch * dec_seq, D_MODEL)
    ck = cache_k.reshape(dec_batch, PAST_LEN * N_HEADS, 2 * HEAD_DIM)
    cv = cache_v.reshape(dec_batch, PAST_LEN * N_HEADS, V_DIM)

    xp, xs = _ffn(xp, xs, mod, gpre, gpost, up1, dn1, sub=0)
    xp, new_k, new_v = _mix_ctx(xp, mod, gpre, gpost, win, convw, lamqk, subg, wo)
    qs, kall, vall, conv = _lat_proj(xs, mod, gpre, win, convw, ck, cv)
    xs = _lat_attn(xs, mod, gpost, qs, kall, vall, conv, lamqk, subg, wo)
    xp, xs = _ffn(xp, xs, mod, gpre, gpost, up2, dn2, sub=2)

    return (xp.reshape(batch, seq, D_MODEL), xs.reshape(dec_batch, dec_seq, D_MODEL),
            new_k.reshape(batch, 1, seq, N_HEADS, 2 * HEAD_DIM), new_v.reshape(batch, 1, seq, N_HEADS, V_DIM))
```

```python
import functools
import math

import jax
import jax.numpy as jnp
import numpy as np
from jax import lax
from jax.experimental import pallas as pl
from jax.experimental.pallas import tpu as pltpu

D_MODEL = 1024
SEQ = 256
DEC_SEQ = 1024
PAST_LEN = 512
GRID_W = 64
ATTN_WIDTH = 512
CONV_WIDTH = 512
N_HEADS = 4
HEAD_DIM = 64
V_DIM = 128
D_FF = 2816
ROPE_BASE = 10000.0
EPS = 1e-6
N_SUB = 3
IN_WIDTH = 3 * ATTN_WIDTH + 3 * CONV_WIDTH
LAMBDA_INIT = 0.8 - 0.6 * math.exp(-0.3 * 0)
QK_SCALE_LOG2 = HEAD_DIM ** -0.5 * math.log2(math.e)

F32 = jnp.float32
BF16 = jnp.bfloat16

MXU_TILE = 256
COND_ROWS = 8
FFN_STEP_ROWS = 1024
FFN_SUB = 512
FFN_CHUNKS = ((0, 1024), (1024, 2048), (2048, D_FF))
W_PIECE = MXU_TILE
N_PIECES = D_FF // W_PIECE
CTX_TM = 512
LAT_QB = 512
VMEM_LIMIT = 60 * 1024 * 1024


def _rms_normalise(x):
    return x * lax.rsqrt(jnp.mean(x * x, axis=-1, keepdims=True) + EPS)


def _dot(a, b):
    return jnp.dot(a, b, preferred_element_type=F32)


def _dot_nt(a, b):
    return lax.dot_general(a, b, (((1,), (1,)), ((), ())), preferred_element_type=F32)


def _const_spec(shape):
    nd = len(shape)
    return pl.BlockSpec(shape, lambda *_: (0,) * nd, pipeline_mode=pl.Buffered(1))


def _mod_kernel(cctx_ref, c_ref, wlo_ref, whi_ref, b_ref, o_ref, cond_ref):
    nlat = c_ref.shape[0]
    half = D_MODEL // 2
    cond_ref[...] = jnp.zeros_like(cond_ref)
    cond_ref[0:1, :] = cctx_ref[...]
    cond_ref[1:1 + nlat, :] = c_ref[...]
    c = cond_ref[...]
    s = (c * jax.nn.sigmoid(c)).astype(BF16)
    o_ref[0, :, :half] = _dot(s, wlo_ref[...].astype(BF16)) + b_ref[:, :half]
    o_ref[0, :, half:] = _dot(s, whi_ref[...].astype(BF16)) + b_ref[:, half:]


def _modulation(c_ctx, c, w_mod, b_mod):
    n_out = w_mod.shape[1]
    half = D_MODEL // 2
    assert n_out == 3 * N_SUB * D_MODEL and 1 + c.shape[0] <= COND_ROWS
    return pl.pallas_call(
        _mod_kernel,
        grid=(3 * N_SUB,),
        in_specs=[
            pl.BlockSpec((1, D_MODEL), lambda j: (0, 0)),
            pl.BlockSpec(c.shape, lambda j: (0, 0)),
            pl.BlockSpec((D_MODEL, half), lambda j: (0, 2 * j)),
            pl.BlockSpec((D_MODEL, half), lambda j: (0, 2 * j + 1)),
            pl.BlockSpec((1, D_MODEL), lambda j: (0, j)),
        ],
        out_specs=pl.BlockSpec((1, COND_ROWS, D_MODEL), lambda j: (j, 0, 0)),
        out_shape=jax.ShapeDtypeStruct((3 * N_SUB, COND_ROWS, D_MODEL), F32),
        scratch_shapes=[pltpu.VMEM((COND_ROWS, D_MODEL), F32)],
        name="modulation",
    )(c_ctx, c, w_mod, w_mod, b_mod)


def _mod_spec():
    return _const_spec((3 * N_SUB, COND_ROWS, D_MODEL))


def _mod_rows(mod_ref, sub, row):
    return tuple(mod_ref[3 * sub + i, pl.ds(row, 1), :] for i in range(3))


def _ffn_kernel(mod_ref, gpre_ref, gpost_ref, xp_hbm, xs_hbm, wup_hbm, wdn_hbm, op_hbm, os_hbm,
                wup_ref, wdn_ref, sa_ref, sb_ref, sd_ref, wsem, xbuf, xsem, obuf, osem, u_ref, acc_ref,
                *, sub, n_ctx, n_lat):
    t = pl.program_id(0)
    n = n_ctx + n_lat
    slot = t % 2
    R, S = FFN_STEP_ROWS, FFN_SUB
    n_sub = R // S

    def tile_dma(k, k_slot, action, into_vmem):
        def go(hbm_ref, blk):
            rows = hbm_ref.at[pl.ds(blk * R, R), :]
            if into_vmem:
                cp = pltpu.make_async_copy(rows, xbuf.at[k_slot], xsem.at[k_slot])
            else:
                cp = pltpu.make_async_copy(obuf.at[k_slot], rows, osem.at[k_slot])
            getattr(cp, action)()
        ctx_ref, lat_ref = (xp_hbm, xs_hbm) if into_vmem else (op_hbm, os_hbm)
        pl.when(k < n_ctx)(lambda: go(ctx_ref, k))
        pl.when(k >= n_ctx)(lambda: go(lat_ref, k - n_ctx))

    def weight_piece(j):
        ps = j % 2
        c = pl.ds(pl.multiple_of(j * W_PIECE, W_PIECE), W_PIECE)
        c2 = pl.ds(pl.multiple_of(D_FF + j * W_PIECE, W_PIECE), W_PIECE)
        return (
            pltpu.make_async_copy(wup_hbm.at[:, c], sa_ref.at[ps], wsem.at[ps, 0]),
            pltpu.make_async_copy(wup_hbm.at[:, c2], sb_ref.at[ps], wsem.at[ps, 1]),
            pltpu.make_async_copy(wdn_hbm.at[c, :], sd_ref.at[ps], wsem.at[ps, 2]),
        )

    def start_piece(j):
        for cp in weight_piece(j):
            cp.start()

    def convert_piece(j, carry):
        for cp in weight_piece(j):
            cp.wait()
        ps = j % 2
        wup_ref[0, j] = sa_ref[ps].astype(BF16)
        wup_ref[1, j] = sb_ref[ps].astype(BF16)
        wdn_ref[j] = sd_ref[ps].astype(BF16)
        pl.when(j + 2 < N_PIECES)(lambda: start_piece(j + 2))
        return carry

    sh, sc, gt = _mod_rows(mod_ref, sub, jnp.where(t < n_ctx, 0, 1 + t - n_ctx))
    pre_gain = gpre_ref[sub:sub + 1, :] * (1.0 + sc)
    post_gain = gpost_ref[sub:sub + 1, :] * (0.5 * gt)

    def pre(s):
        x = xbuf[slot, s * S:(s + 1) * S, :]
        u_ref[s] = (_rms_normalise(x) * pre_gain + sh).astype(BF16)

    def mm(s, ci):
        lo, hi = FFN_CHUNKS[ci]
        pieces = range(lo // W_PIECE, hi // W_PIECE)
        ub = u_ref[s]
        a = jnp.concatenate([_dot(ub, wup_ref[0, j]) for j in pieces], axis=1)
        b = jnp.concatenate([_dot(ub, wup_ref[1, j]) for j in pieces], axis=1)
        h = (a * jax.nn.sigmoid(a) * b).astype(BF16)
        part = _dot(h, wdn_ref[pieces.start:pieces.stop].reshape(hi - lo, D_MODEL))
        if ci == 0:
            acc_ref[s] = part
        else:
            acc_ref[s] += part

    def post(s):
        x = xbuf[slot, s * S:(s + 1) * S, :]
        obuf[slot, s * S:(s + 1) * S, :] = x + _rms_normalise(acc_ref[s]) * post_gain

    @pl.when(t == 0)
    def _():
        tile_dma(t, slot, "start", True)
        start_piece(0)
        start_piece(1)

    tile_dma(t, slot, "wait", True)
    pl.when(t + 1 < n)(lambda: tile_dma(t + 1, 1 - slot, "start", True))
    pl.when(t >= 2)(lambda: tile_dma(t - 2, slot, "wait", False))

    @pl.when(t == 0)
    def _():
        lax.fori_loop(0, N_PIECES, convert_piece, 0)

    for s in range(n_sub):
        pre(s)
    for ci in range(len(FFN_CHUNKS)):
        for s in range(n_sub):
            mm(s, ci)
    for s in range(n_sub):
        post(s)

    tile_dma(t, slot, "start", False)

    @pl.when(t == n - 1)
    def _():
        if n >= 2:
            tile_dma(t - 1, 1 - slot, "wait", False)
        tile_dma(t, slot, "wait", False)


def _ffn(xp2d, xs2d, mod3, gpre, gpost, wup, wdn, *, sub):
    R = FFN_STEP_ROWS
    n_ctx, n_lat = xp2d.shape[0] // R, xs2d.shape[0] // R
    assert R == DEC_SEQ and xp2d.shape[0] % R == 0 and xs2d.shape[0] % R == 0
    any_spec = pl.BlockSpec(memory_space=pl.ANY)
    return pl.pallas_call(
        functools.partial(_ffn_kernel, sub=sub, n_ctx=n_ctx, n_lat=n_lat),
        grid=(n_ctx + n_lat,),
        in_specs=[
            _mod_spec(),
            _const_spec((N_SUB, D_MODEL)),
            _const_spec((N_SUB, D_MODEL)),
            any_spec, any_spec, any_spec, any_spec,
        ],
        out_specs=[any_spec, any_spec],
        out_shape=[jax.ShapeDtypeStruct(xp2d.shape, F32), jax.ShapeDtypeStruct(xs2d.shape, F32)],
        scratch_shapes=[
            pltpu.VMEM((2, N_PIECES, D_MODEL, W_PIECE), BF16),
            pltpu.VMEM((N_PIECES, W_PIECE, D_MODEL), BF16),
            pltpu.VMEM((2, D_MODEL, W_PIECE), F32),
            pltpu.VMEM((2, D_MODEL, W_PIECE), F32),
            pltpu.VMEM((2, W_PIECE, D_MODEL), F32),
            pltpu.SemaphoreType.DMA((2, 3)),
            pltpu.VMEM((2, R, D_MODEL), F32),
            pltpu.SemaphoreType.DMA((2,)),
            pltpu.VMEM((2, R, D_MODEL), F32),
            pltpu.SemaphoreType.DMA((2,)),
            pltpu.VMEM((R // FFN_SUB, FFN_SUB, D_MODEL), BF16),
            pltpu.VMEM((R // FFN_SUB, FFN_SUB, D_MODEL), F32),
        ],
        compiler_params=pltpu.CompilerParams(dimension_semantics=("arbitrary",), vmem_limit_bytes=VMEM_LIMIT),
        name=f"ffn{sub}",
    )(mod3, gpre, gpost, xp2d, xs2d, wup, wdn)


def _lambda(lam_ref):
    lq = lam_ref[...]
    l1 = jnp.sum(lq[0:1, :] * lq[1:2, :], axis=-1, keepdims=True)
    l2 = jnp.sum(lq[2:3, :] * lq[3:4, :], axis=-1, keepdims=True)
    return jnp.exp(l1) - jnp.exp(l2) + LAMBDA_INIT


def _split_maps(qh):
    lane = lax.broadcasted_iota(jnp.int32, qh.shape, 1)
    q0 = jnp.where(lane < HEAD_DIM, qh, 0.0).astype(BF16)
    q1 = jnp.where(lane >= HEAD_DIM, qh, 0.0).astype(BF16)
    return q0, q1


def _attn_scores(s_ref, idx, qs, kh):
    s_ref[idx] = _dot_nt(qs, kh)


def _attn_probs(s_ref, pd_ref, rz_ref, idx, lam):
    s = s_ref[idx]
    m = s.shape[0] // 2
    e = jnp.exp2(s - jnp.max(s, axis=-1, keepdims=True))
    z = jnp.sum(e, axis=-1, keepdims=True)
    c = lam * z[:m] / z[m:]
    pd_ref[idx] = (e[:m] - c * e[m:]).astype(BF16)
    rz_ref[idx] = jnp.broadcast_to(1.0 / z[:m], (m, V_DIM))


def _attn_values(pd_ref, rz_ref, idx, vh, head_gain):
    o = _dot(pd_ref[idx], vh) * rz_ref[idx]
    return (_rms_normalise(o) * head_gain).astype(BF16)


def _short_conv(bg, cg, xc, convw_ref, seq):
    g = cg * xc
    m = g.shape[0]
    row = lax.broadcasted_iota(jnp.int32, g.shape, 0) % seq
    g_prev = jnp.where(row == 0, 0.0, pltpu.roll(g, 1, 0))
    g_next = jnp.where(row == seq - 1, 0.0, pltpu.roll(g, m - 1, 0))
    conv = g_prev * convw_ref[0:1, :] + g * convw_ref[1:2, :] + g_next * convw_ref[2:3, :]
    return bg * conv


def _pre_mod(x, mod_ref, gpre_ref, sub, row):
    sh, sc, _ = _mod_rows(mod_ref, sub, row)
    return _rms_normalise(x) * (gpre_ref[sub:sub + 1, :] * (1.0 + sc)) + sh


def _mix_ctx_kernel(x_ref, mod_ref, gpre_ref, gpost_ref, win_ref, convw_ref, lam_ref, subg_ref, wo_ref,
                    o_ref, k_ref, v_ref, u_ref, q_ref, kb_ref, vb_ref, cat_ref, s_ref, pd_ref, rz_ref, cv_ref,
                    mixc_ref):
    A = ATTN_WIDTH
    x = x_ref[...]
    u_ref[...] = _pre_mod(x, mod_ref, gpre_ref, 1, 0).astype(BF16)
    q_ref[...] = _dot(u_ref[...], win_ref[:, 0:A].astype(BF16)) * QK_SCALE_LOG2
    k = _dot(u_ref[...], win_ref[:, A:2 * A].astype(BF16))
    v = _dot(u_ref[...], win_ref[:, 2 * A:3 * A].astype(BF16))
    kb_ref[...] = k.astype(BF16)
    vb_ref[...] = v.astype(BF16)
    for h in range(N_HEADS):
        k_ref[pl.ds(h, CTX_TM, stride=N_HEADS), :] = k[:, h * V_DIM:(h + 1) * V_DIM]
        v_ref[pl.ds(h, CTX_TM, stride=N_HEADS), :] = v[:, h * V_DIM:(h + 1) * V_DIM]
    lam = _lambda(lam_ref)
    head_gain = subg_ref[...] * (1.0 - LAMBDA_INIT)
    n_units = (CTX_TM // SEQ) * N_HEADS
    n_conv_pieces = 3 * CONV_WIDTH // MXU_TILE

    def unit_slices(uid):
        b, h = divmod(uid, N_HEADS)
        return slice(b * SEQ, (b + 1) * SEQ), slice(h * V_DIM, (h + 1) * V_DIM)

    def conv_piece(i):
        c0 = 3 * A + i * MXU_TILE
        cv_ref[:, i * MXU_TILE:(i + 1) * MXU_TILE] = _dot(u_ref[...], win_ref[:, c0:c0 + MXU_TILE].astype(BF16))

    def mix_conv_piece(j):
        cols = slice(j * MXU_TILE, (j + 1) * MXU_TILE)
        mixc_ref[:, cols] = _dot(cat_ref[:, A:], wo_ref[A:, cols].astype(BF16))

    for uid in range(n_units):
        rows, cols = unit_slices(uid)
        q0, q1 = _split_maps(q_ref[rows, cols])
        _attn_scores(s_ref, uid, jnp.concatenate([q0, q1], axis=0), kb_ref[rows, cols])
    for uid in range(n_units):
        _attn_probs(s_ref, pd_ref, rz_ref, uid, lam)
        if uid < n_conv_pieces:
            conv_piece(uid)
    bg = cv_ref[:, 0:CONV_WIDTH]
    cg = cv_ref[:, CONV_WIDTH:2 * CONV_WIDTH]
    xc = cv_ref[:, 2 * CONV_WIDTH:]
    cat_ref[:, A:] = _short_conv(bg, cg, xc, convw_ref, SEQ).astype(BF16)
    for uid in range(n_units):
        rows, cols = unit_slices(uid)
        cat_ref[rows, cols] = _attn_values(pd_ref, rz_ref, uid, vb_ref[rows, cols], head_gain)
        if uid % 2 == 1:
            mix_conv_piece(uid // 2)
    mix = mixc_ref[...] + _dot(cat_ref[:, :A], wo_ref[:A, :].astype(BF16))
    gt = _mod_rows(mod_ref, 1, 0)[2]
    o_ref[...] = x + _rms_normalise(mix) * (gpost_ref[1:2, :] * gt)


def _mix_ctx(x2d, mod3, gpre, gpost, win, convw, lamqk, subg, wo):
    n = x2d.shape[0]
    tm = CTX_TM
    n_units = (tm // SEQ) * N_HEADS
    assert 3 * CONV_WIDTH // MXU_TILE <= n_units and D_MODEL // MXU_TILE == n_units // 2
    row = pl.BlockSpec((tm, D_MODEL), lambda t: (t, 0))
    kv = pl.BlockSpec((tm * N_HEADS, V_DIM), lambda t: (t, 0))
    return pl.pallas_call(
        _mix_ctx_kernel,
        grid=(n // tm,),
        in_specs=[
            row,
            _mod_spec(),
            _const_spec((N_SUB, D_MODEL)),
            _const_spec((N_SUB, D_MODEL)),
            _const_spec((D_MODEL, IN_WIDTH)),
            _const_spec((3, CONV_WIDTH)),
            _const_spec((4, HEAD_DIM)),
            _const_spec((1, V_DIM)),
            _const_spec((D_MODEL, D_MODEL)),
        ],
        out_specs=[row, kv, kv],
        out_shape=[
            jax.ShapeDtypeStruct((n, D_MODEL), F32),
            jax.ShapeDtypeStruct((n * N_HEADS, V_DIM), F32),
            jax.ShapeDtypeStruct((n * N_HEADS, V_DIM), F32),
        ],
        scratch_shapes=[
            pltpu.VMEM((tm, D_MODEL), BF16),
            pltpu.VMEM((tm, ATTN_WIDTH), F32),
            pltpu.VMEM((tm, ATTN_WIDTH), BF16),
            pltpu.VMEM((tm, ATTN_WIDTH), BF16),
            pltpu.VMEM((tm, D_MODEL), BF16),
            pltpu.VMEM((n_units, 2 * SEQ, SEQ), F32),
            pltpu.VMEM((n_units, SEQ, SEQ), BF16),
            pltpu.VMEM((n_units, SEQ, V_DIM), F32),
            pltpu.VMEM((tm, 3 * CONV_WIDTH), F32),
            pltpu.VMEM((tm, D_MODEL), F32),
        ],
        compiler_params=pltpu.CompilerParams(vmem_limit_bytes=VMEM_LIMIT),
        name="mixer_ctx",
    )(x2d, mod3, gpre, gpost, win, convw, lamqk, subg, wo)


def _rope(xh, cos, sin_lo, sin_hi):
    return xh * cos + pltpu.roll(xh, 16, 1) * sin_hi + pltpu.roll(xh, V_DIM - 16, 1) * sin_lo


def _rope_tables():
    t = np.arange(DEC_SEQ)
    row = (t // GRID_W).astype(np.float64)
    col = (t % GRID_W).astype(np.float64)
    half = HEAD_DIM // 2
    freqs = ROPE_BASE ** (-np.arange(0, half, 2, dtype=np.float64) / half)
    ang = np.concatenate([row[:, None] * freqs, row[:, None] * freqs,
                          col[:, None] * freqs, col[:, None] * freqs], axis=-1)
    cos, sin = np.cos(ang), np.sin(ang)
    first = ((np.arange(HEAD_DIM) % half) < half // 2)[None, :]
    sin_lo = np.where(first, -sin, 0.0)
    sin_hi = np.where(first, 0.0, sin)
    tile = lambda a: jnp.asarray(np.concatenate([a, a], axis=-1), dtype=F32)
    return tile(cos), tile(sin_lo), tile(sin_hi)


def _lat_proj_kernel(x_ref, mod_ref, gpre_ref, win_ref, convw_ref, ck_ref, cv_ref, cos_ref, slo_ref, shi_ref,
                     qs_ref, kall_ref, vall_ref, conv_ref, u_ref):
    A, T, L = ATTN_WIDTH, DEC_SEQ, PAST_LEN
    u_ref[...] = _pre_mod(x_ref[...], mod_ref, gpre_ref, 1, 1 + pl.program_id(0)).astype(BF16)
    cos, slo, shi = cos_ref[...], slo_ref[...], shi_ref[...]
    q = _dot(u_ref[...], win_ref[:, 0:A].astype(BF16))
    for h in range(N_HEADS):
        cols = slice(h * V_DIM, (h + 1) * V_DIM)
        q0, q1 = _split_maps(_rope(q[:, cols], cos, slo, shi) * QK_SCALE_LOG2)
        for qb in range(T // LAT_QB):
            rows = slice(qb * LAT_QB, (qb + 1) * LAT_QB)
            qs_ref[0, h, qb, 0:LAT_QB, :] = q0[rows]
            qs_ref[0, h, qb, LAT_QB:, :] = q1[rows]
    k = _dot(u_ref[...], win_ref[:, A:2 * A].astype(BF16))
    for h in range(N_HEADS):
        cols = slice(h * V_DIM, (h + 1) * V_DIM)
        kall_ref[0, h, 0:L, :] = ck_ref[0, pl.ds(h, L, stride=N_HEADS), :].astype(BF16)
        kall_ref[0, h, L:, :] = _rope(k[:, cols], cos, slo, shi).astype(BF16)
    v = _dot(u_ref[...], win_ref[:, 2 * A:3 * A].astype(BF16))
    for h in range(N_HEADS):
        cols = slice(h * V_DIM, (h + 1) * V_DIM)
        vall_ref[0, h, 0:L, :] = cv_ref[0, pl.ds(h, L, stride=N_HEADS), :].astype(BF16)
        vall_ref[0, h, L:, :] = v[:, cols].astype(BF16)
    ub = u_ref[...]
    bg = _dot(ub, win_ref[:, 3 * A:3 * A + CONV_WIDTH].astype(BF16))
    cg = _dot(ub, win_ref[:, 3 * A + CONV_WIDTH:3 * A + 2 * CONV_WIDTH].astype(BF16))
    xc = _dot(ub, win_ref[:, 3 * A + 2 * CONV_WIDTH:].astype(BF16))
    conv_ref[...] = _short_conv(bg, cg, xc, convw_ref, T).astype(BF16)


def _lat_proj(x2d, mod3, gpre, win, convw, ck, cv):
    n = x2d.shape[0]
    T, L = DEC_SEQ, PAST_LEN
    nb = n // T
    nqb = T // LAT_QB
    cos, slo, shi = _rope_tables()
    cache = pl.BlockSpec((1, L * N_HEADS, V_DIM), lambda b: (b, 0, 0))
    return pl.pallas_call(
        _lat_proj_kernel,
        grid=(nb,),
        in_specs=[
            pl.BlockSpec((T, D_MODEL), lambda b: (b, 0)),
            _mod_spec(),
            _const_spec((N_SUB, D_MODEL)),
            _const_spec((D_MODEL, IN_WIDTH)),
            _const_spec((3, CONV_WIDTH)),
            cache, cache,
            _const_spec((T, V_DIM)), _const_spec((T, V_DIM)), _const_spec((T, V_DIM)),
        ],
        out_specs=[
            pl.BlockSpec((1, N_HEADS, nqb, 2 * LAT_QB, V_DIM), lambda b: (b, 0, 0, 0, 0)),
            pl.BlockSpec((1, N_HEADS, L + T, V_DIM), lambda b: (b, 0, 0, 0)),
            pl.BlockSpec((1, N_HEADS, L + T, V_DIM), lambda b: (b, 0, 0, 0)),
            pl.BlockSpec((T, CONV_WIDTH), lambda b: (b, 0)),
        ],
        out_shape=[
            jax.ShapeDtypeStruct((nb, N_HEADS, nqb, 2 * LAT_QB, V_DIM), BF16),
            jax.ShapeDtypeStruct((nb, N_HEADS, L + T, V_DIM), BF16),
            jax.ShapeDtypeStruct((nb, N_HEADS, L + T, V_DIM), BF16),
            jax.ShapeDtypeStruct((n, CONV_WIDTH), BF16),
        ],
        scratch_shapes=[pltpu.VMEM((T, D_MODEL), BF16)],
        compiler_params=pltpu.CompilerParams(vmem_limit_bytes=VMEM_LIMIT),
        name="mixer_lat_proj",
    )(x2d, mod3, gpre, win, convw, ck, cv, cos, slo, shi)


def _lat_attn_kernel(x_ref, mod_ref, gpost_ref, qs_ref, kall_ref, vall_ref, conv_ref, lam_ref, subg_ref, wo_ref,
                     o_ref, cat_ref, s_ref, pd_ref, rz_ref):
    A = ATTN_WIDTH
    lam = _lambda(lam_ref)
    head_gain = subg_ref[...] * (1.0 - LAMBDA_INIT)

    def scores(h):
        _attn_scores(s_ref, h % 2, qs_ref[0, h, 0], kall_ref[0, h])

    def probs(h):
        _attn_probs(s_ref, pd_ref, rz_ref, h % 2, lam)

    def values(h):
        cat_ref[:, h * V_DIM:(h + 1) * V_DIM] = _attn_values(pd_ref, rz_ref, h % 2, vall_ref[0, h], head_gain)

    scores(0)
    scores(1)
    mix_conv = _dot(conv_ref[...], wo_ref[A:, :].astype(BF16))
    for h in range(N_HEADS):
        probs(h)
        if h + 2 < N_HEADS:
            scores(h + 2)
        if h >= 1:
            values(h - 1)
    values(N_HEADS - 1)
    mix = mix_conv + _dot(cat_ref[...], wo_ref[:A, :].astype(BF16))
    gt = _mod_rows(mod_ref, 1, 1 + pl.program_id(0))[2]
    o_ref[...] = x_ref[...] + _rms_normalise(mix) * (gpost_ref[1:2, :] * gt)


def _lat_attn(x2d, mod3, gpost, qs, kall, vall, conv, lamqk, subg, wo):
    n = x2d.shape[0]
    T, L = DEC_SEQ, PAST_LEN
    nb = n // T
    nqb = T // LAT_QB
    row = pl.BlockSpec((LAT_QB, D_MODEL), lambda b, i: (b * nqb + i, 0))
    keys = pl.BlockSpec((1, N_HEADS, L + T, V_DIM), lambda b, i: (b, 0, 0, 0))
    return pl.pallas_call(
        _lat_attn_kernel,
        grid=(nb, nqb),
        in_specs=[
            row,
            _mod_spec(),
            _const_spec((N_SUB, D_MODEL)),
            pl.BlockSpec((1, N_HEADS, 1, 2 * LAT_QB, V_DIM), lambda b, i: (b, 0, i, 0, 0)),
            keys, keys,
            pl.BlockSpec((LAT_QB, CONV_WIDTH), lambda b, i: (b * nqb + i, 0)),
            _const_spec((4, HEAD_DIM)),
            _const_spec((1, V_DIM)),
            _const_spec((D_MODEL, D_MODEL)),
        ],
        out_specs=row,
        out_shape=jax.ShapeDtypeStruct((n, D_MODEL), F32),
        scratch_shapes=[
            pltpu.VMEM((LAT_QB, ATTN_WIDTH), BF16),
            pltpu.VMEM((2, 2 * LAT_QB, L + T), F32),
            pltpu.VMEM((2, LAT_QB, L + T), BF16),
            pltpu.VMEM((2, LAT_QB, V_DIM), F32),
        ],
        compiler_params=pltpu.CompilerParams(vmem_limit_bytes=VMEM_LIMIT),
        name="mixer_lat_attn",
    )(x2d, mod3, gpost, qs, kall, vall, conv, lamqk, subg, wo)


def kernel(x_prompt, x_sample, c, cache_k, cache_v, c_ctx, w_mod, b_mod, norm_pre, norm_post,
           ffn1_up, ffn1_down, ffn2_up, ffn2_down, w_in, conv_w, lam_qk, subln_g, w_o):
    batch, seq, _ = x_prompt.shape
    dec_batch, dec_seq, _ = x_sample.shape
    assert (seq, dec_seq) == (SEQ, DEC_SEQ) and cache_k.shape[1] == 1

    mod = _modulation(c_ctx.reshape(1, D_MODEL), c, w_mod[0], b_mod)

    gpre, gpost = norm_pre[0], norm_post[0]
    up1, dn1, up2, dn2 = ffn1_up[0], ffn1_down[0], ffn2_up[0], ffn2_down[0]
    win, wo = w_in[0], w_o[0]
    convw, lamqk, subg = conv_w[0], lam_qk[0], subln_g

    xp = x_prompt.reshape(batch * seq, D_MODEL)
    xs = x_sample.reshape(dec_batch * dec_seq, D_MODEL)
    ck = cache_k.reshape(dec_batch, PAST_LEN * N_HEADS, 2 * HEAD_DIM)
    cv = cache_v.reshape(dec_batch, PAST_LEN * N_HEADS, V_DIM)

    xp, xs = _ffn(xp, xs, mod, gpre, gpost, up1, dn1, sub=0)
    xp, new_k, new_v = _mix_ctx(xp, mod, gpre, gpost, win, convw, lamqk, subg, wo)
    qs, kall, vall, conv = _lat_proj(xs, mod, gpre, win, convw, ck, cv)
    xs = _lat_attn(xs, mod, gpost, qs, kall, vall, conv, lamqk, subg, wo)
    xp, xs = _ffn(xp, xs, mod, gpre, gpost, up2, dn2, sub=2)

    return (xp.reshape(batch, seq, D_MODEL), xs.reshape(dec_batch, dec_seq, D_MODEL),
            new_k.reshape(batch, 1, seq, N_HEADS, 2 * HEAD_DIM), new_v.reshape(batch, 1, seq, N_HEADS, V_DIM))
```
